```python
import math
import jax, jax.numpy as jnp
from jax import lax
import numpy as np

D_MODEL = 2048
BATCH = 4
SEQ = 4096
DEPTH = 2

CTX_LEN = 256
GRID_W = 64

ATT_HEADS = 16
ATT_KV_HEADS = 4
ATT_HEAD_DIM = 64
ATT_GROUP = ATT_HEADS // ATT_KV_HEADS
ATT_WINDOW = 128
ATT_BLOCK = 128
ROPE_BASE = 10000.0
ATT_Q_W = ATT_HEADS * ATT_HEAD_DIM
ATT_KV_W = ATT_KV_HEADS * ATT_HEAD_DIM

DN_HEADS = 4
DN_HEAD_DIM = 128
DN_W = DN_HEADS * DN_HEAD_DIM
DN_CHUNK = 64

LRU_WIDTH = 512
LRU_BLOCKS = 8
LRU_BLOCK_DIM = LRU_WIDTH // LRU_BLOCKS
LRU_C = 8.0

CONV_W = 4
CONV_LEFT = CONV_W // 2

N_EXPERTS = 32
TOP_K = 4
D_EXPERT = 1024
SWIGLU_LIMIT = 7.0
SWIGLU_ALPHA = 1.702
MOE_BLOCK = 256

MIX_W = ATT_Q_W + DN_W + LRU_WIDTH
IN_SPLITS = (ATT_Q_W, ATT_KV_W, ATT_KV_W, DN_W, DN_W, DN_W, DN_W,
             2 * DN_HEADS, 2 * DN_HEADS, LRU_WIDTH, LRU_WIDTH)
IN_W = sum(IN_SPLITS)
EPS = 1e-6
NEG_INF = -1e30

kernel_name = 'hybrid_lru_deltanet_swa_moe_prefix_dit'


def _rmsnorm(x, g):
    xf = x.astype(jnp.float32)
    y = xf * lax.rsqrt(jnp.mean(xf * xf, axis=-1, keepdims=True) + EPS)
    return (y * g.astype(jnp.float32)).astype(x.dtype)


def _modulate(x, g, shift, scale):
    return _rmsnorm(x, g) * (1.0 + scale) + shift


def _centred_dwconv(x, w):
    t = x.shape[1]
    xp = jnp.pad(x, ((0, 0), (CONV_LEFT, CONV_W - 1 - CONV_LEFT), (0, 0)))
    return sum(xp[:, j:j + t] * w[j] for j in range(CONV_W))


def _l2norm(x):
    return x * lax.rsqrt(jnp.sum(x * x, axis=-1, keepdims=True) + EPS)


def _rotate(xa, pos):
    r = xa.shape[-1]
    inv = ROPE_BASE ** (-jnp.arange(0, r, 2, dtype=jnp.float32) / r)
    ang = pos.astype(jnp.float32)[:, None] * inv
    cos = jnp.cos(ang)[None, :, None, :].astype(xa.dtype)
    sin = jnp.sin(ang)[None, :, None, :].astype(xa.dtype)
    x1, x2 = xa[..., :r // 2], xa[..., r // 2:]
    return jnp.concatenate([x1 * cos - x2 * sin, x2 * cos + x1 * sin], axis=-1)


def _axial_rope(x, row, col):
    half = x.shape[-1] // 2
    return jnp.concatenate([_rotate(x[..., :half], row), _rotate(x[..., half:], col)], axis=-1)


def _sink_probs(logits, sink):
    m = jnp.maximum(jnp.max(logits, axis=-1, keepdims=True), sink)
    p = jnp.exp(logits - m)
    return p / (jnp.sum(p, axis=-1, keepdims=True) + jnp.exp(sink - m))


def _window_attention(q, k, v, kc, vc, sink_b):
    b, s = q.shape[:2]
    nblk = s // ATT_BLOCK
    scale = ATT_HEAD_DIM ** -0.5
    qb = jnp.moveaxis(q.reshape(b, nblk, ATT_BLOCK, ATT_KV_HEADS, ATT_GROUP, ATT_HEAD_DIM), 1, 0)
    pad = ((0, 0), (ATT_BLOCK, ATT_BLOCK), (0, 0), (0, 0))
    kp = jnp.pad(k, pad)
    vp = jnp.pad(v, pad)
    span = 3 * ATT_BLOCK
    rel = (jnp.arange(span)[None, :] - ATT_BLOCK) - jnp.arange(ATT_BLOCK)[:, None]

    def block(args):
        qi, i = args
        start = i * ATT_BLOCK
        ki = lax.dynamic_slice_in_dim(kp, start, span, axis=1)
        vi = lax.dynamic_slice_in_dim(vp, start, span, axis=1)
        kabs = start - ATT_BLOCK + jnp.arange(span)
        valid = (jnp.abs(rel) <= ATT_WINDOW) & (kabs >= 0)[None, :] & (kabs < s)[None, :]
        lg_l = jnp.einsum('bqkgd,bnkd->bkgqn', qi, ki, preferred_element_type=jnp.float32) * scale
        lg_l = jnp.where(valid, lg_l, NEG_INF)
        lg_c = jnp.einsum('bqkgd,bckd->bkgqc', qi, kc, preferred_element_type=jnp.float32) * scale
        p = _sink_probs(jnp.concatenate([lg_l, lg_c], axis=-1), sink_b).astype(v.dtype)
        return (jnp.einsum('bkgqn,bnkd->bqkgd', p[..., :span], vi)
                + jnp.einsum('bkgqc,bckd->bqkgd', p[..., span:], vc))

    o = lax.map(block, (qb, jnp.arange(nblk)))
    return jnp.moveaxis(o, 0, 1).reshape(b, s, ATT_Q_W)


def _context_attention(qc, kc, vc, sink_b):
    b, l = qc.shape[:2]
    lg = jnp.einsum('bqkgd,bckd->bkgqc', qc, kc, preferred_element_type=jnp.float32) * ATT_HEAD_DIM ** -0.5
    p = _sink_probs(lg, sink_b).astype(vc.dtype)
    return jnp.einsum('bkgqc,bckd->bqkgd', p, vc).reshape(b, l, ATT_Q_W)


def _attention_mixer(lat, ctxp, sink, row, col, need_ctx):
    ql, kl, vl = lat
    qc, kc, vc = ctxp
    b, s = ql.shape[:2]
    l = qc.shape[1]
    ql = _axial_rope(ql.reshape(b, s, ATT_HEADS, ATT_HEAD_DIM), row, col)
    kl = _axial_rope(kl.reshape(b, s, ATT_KV_HEADS, ATT_HEAD_DIM), row, col)
    vl = vl.reshape(b, s, ATT_KV_HEADS, ATT_HEAD_DIM)
    kc = kc.reshape(b, l, ATT_KV_HEADS, ATT_HEAD_DIM)
    vc = vc.reshape(b, l, ATT_KV_HEADS, ATT_HEAD_DIM)
    sink_b = sink.astype(jnp.float32).reshape(ATT_KV_HEADS, ATT_GROUP)[None, :, :, None, None]
    out_l = _window_attention(ql, kl, vl, kc, vc, sink_b)
    out_c = None
    if need_ctx:
        out_c = _context_attention(qc.reshape(b, l, ATT_KV_HEADS, ATT_GROUP, ATT_HEAD_DIM), kc, vc, sink_b)
    return out_l, out_c


def _gated_delta_chunked(q, k, v, beta, g, s0):
    b, t, h, dk = q.shape
    dv = v.shape[-1]
    n = t // DN_CHUNK

    def chunks(a):
        a = a.reshape((b, n, DN_CHUNK) + a.shape[2:])
        return jnp.moveaxis(jnp.moveaxis(a, 3, 2), 1, 0)

    qc, kc, vc, bc, gc = chunks(q), chunks(k), chunks(v), chunks(beta), chunks(g)
    gcum = jnp.cumsum(gc, axis=-1)
    tri_incl = jnp.tril(jnp.ones((DN_CHUNK, DN_CHUNK), bool))
    tri_strict = jnp.tril(jnp.ones((DN_CHUNK, DN_CHUNK), bool), -1)
    decay = jnp.exp(jnp.where(tri_incl, gcum[..., :, None] - gcum[..., None, :], -jnp.inf))
    kb = kc * bc[..., None]
    a_strict = jnp.where(tri_strict, jnp.einsum('nbhrd,nbhjd->nbhrj', kb, kc) * decay, 0.0)
    rhs = jnp.concatenate([vc * bc[..., None], kb * jnp.exp(gcum)[..., None]], axis=-1)
    sol = lax.linalg.triangular_solve(a_strict, rhs, left_side=True, lower=True, unit_diagonal=True)
    u, w = sol[..., :dv], sol[..., dv:]
    qk = jnp.einsum('nbhrd,nbhjd->nbhrj', qc, kc) * decay

    def step(state, xs):
        qk_i, q_i, k_i, u_i, w_i, g_i = xs
        v_new = u_i - jnp.einsum('bhcd,bhde->bhce', w_i, state)
        o = (jnp.einsum('bhcd,bhde->bhce', q_i * jnp.exp(g_i)[..., None], state)
             + jnp.einsum('bhcj,bhje->bhce', qk_i, v_new))
        g_last = g_i[..., -1:]
        state = (state * jnp.exp(g_last)[..., None]
                 + jnp.einsum('bhcd,bhce->bhde', k_i * jnp.exp(g_last - g_i)[..., None], v_new))
        return state, o

    s_fin, o = lax.scan(step, s0, (qk, qc, kc, u, w, gcum))
    o = jnp.moveaxis(jnp.moveaxis(o, 0, 1), 2, 3).reshape(b, t, h, dv)
    return o, s_fin


def _deltanet_prep(pq, pk, pv, pb, pa, conv_w, a_log, dt_bias):
    b, t = pq.shape[:2]
    qkv = jax.nn.silu(_centred_dwconv(jnp.concatenate([pq, pk, pv], axis=-1), conv_w)).astype(jnp.float32)
    q, k, v = jnp.split(qkv, 3, axis=-1)
    q = _l2norm(q.reshape(b, t, DN_HEADS, DN_HEAD_DIM)) * DN_HEAD_DIM ** -0.5
    k = _l2norm(k.reshape(b, t, DN_HEADS, DN_HEAD_DIM))
    v = v.reshape(b, t, DN_HEADS, DN_HEAD_DIM)
    beta = jax.nn.sigmoid(pb.astype(jnp.float32)).reshape(b, t, 2, DN_HEADS)
    g = (-jnp.exp(a_log.astype(jnp.float32))
         * jax.nn.softplus(pa.astype(jnp.float32).reshape(b, t, 2, DN_HEADS) + dt_bias.astype(jnp.float32)))
    return q, k, v, beta, g


def _deltanet_gated_out(o, gate, norm_g, dtype):
    b, t = o.shape[:2]
    y = o * lax.rsqrt(jnp.mean(o * o, axis=-1, keepdims=True) + EPS) * norm_g.astype(jnp.float32)
    y = y * jax.nn.silu(gate.astype(jnp.float32).reshape(b, t, DN_HEADS, DN_HEAD_DIM))
    return y.reshape(b, t, DN_W).astype(dtype)


def _deltanet_mixer(lat, ctxp, conv_w, a_log, dt_bias, norm_g, need_ctx):
    ql, kl, vl, bl, gl = _deltanet_prep(lat[0], lat[1], lat[2], lat[4], lat[5], conv_w, a_log, dt_bias)
    qc, kc, vc, bc, gc = _deltanet_prep(ctxp[0], ctxp[1], ctxp[2], ctxp[4], ctxp[5], conv_w, a_log, dt_bias)
    s0 = jnp.zeros((ql.shape[0], DN_HEADS, DN_HEAD_DIM, DN_HEAD_DIM), jnp.float32)
    fl = lambda a: jnp.flip(a, axis=1)
    oc_f, sc_f = _gated_delta_chunked(qc, kc, vc, bc[:, :, 0], gc[:, :, 0], s0)
    ol_f, _ = _gated_delta_chunked(ql, kl, vl, bl[:, :, 0], gl[:, :, 0], sc_f)
    oc_b, sc_b = _gated_delta_chunked(fl(qc), fl(kc), fl(vc), fl(bc[:, :, 1]), fl(gc[:, :, 1]), s0)
    ol_b, _ = _gated_delta_chunked(fl(ql), fl(kl), fl(vl), fl(bl[:, :, 1]), fl(gl[:, :, 1]), sc_b)
    out_l = _deltanet_gated_out(ol_f + fl(ol_b), lat[3], norm_g, lat[0].dtype)
    out_c = _deltanet_gated_out(oc_f + fl(oc_b), ctxp[3], norm_g, ctxp[0].dtype) if need_ctx else None
    return out_l, out_c


def _rglru_coeffs(xc, w_r, b_r, w_i, b_i, lam):
    b, t = xc.shape[:2]
    xr = xc.reshape(b, t, LRU_BLOCKS, LRU_BLOCK_DIM)
    r = jax.nn.sigmoid(jnp.einsum('btni,nij->btnj', xr, w_r.astype(jnp.float32)).reshape(b, t, LRU_WIDTH) + b_r)
    i = jax.nn.sigmoid(jnp.einsum('btni,nij->btnj', xr, w_i.astype(jnp.float32)).reshape(b, t, LRU_WIDTH) + b_i)
    log_a = LRU_C * r * jax.nn.log_sigmoid(lam.astype(jnp.float32))
    return jnp.exp(log_a), jnp.sqrt(-jnp.expm1(2.0 * log_a)) * (i * xc)


def _linear_scan(a, u, h0):
    def combine(lhs, rhs):
        return lhs[0] * rhs[0], rhs[0] * lhs[1] + rhs[1]
    a_cum, u_cum = lax.associative_scan(combine, (a, u), axis=1)
    return a_cum * h0[:, None, :] + u_cum


def _ctx_then_latent(ac, uc, al, ul, h0):
    hc = _linear_scan(ac, uc, h0)
    return hc, _linear_scan(al, ul, hc[:, -1])


def _lru_mixer(lat, ctxp, conv_w, conv_b, w_r, b_r, w_i, b_i, lam, need_ctx):
    xl, yl = lat
    xc, yc = ctxp
    ul_in = (_centred_dwconv(xl, conv_w) + conv_b).astype(jnp.float32)
    uc_in = (_centred_dwconv(xc, conv_w) + conv_b).astype(jnp.float32)
    h0 = jnp.zeros((xl.shape[0], LRU_WIDTH), jnp.float32)
    fl = lambda a: jnp.flip(a, axis=1)
    acf, ucf = _rglru_coeffs(uc_in, w_r[0], b_r[0], w_i[0], b_i[0], lam[0])
    alf, ulf = _rglru_coeffs(ul_in, w_r[0], b_r[0], w_i[0], b_i[0], lam[0])
    hc_f, hl_f = _ctx_then_latent(acf, ucf, alf, ulf, h0)
    acb, ucb = _rglru_coeffs(uc_in, w_r[1], b_r[1], w_i[1], b_i[1], lam[1])
    alb, ulb = _rglru_coeffs(ul_in, w_r[1], b_r[1], w_i[1], b_i[1], lam[1])
    hc_b, hl_b = _ctx_then_latent(fl(acb), fl(ucb), fl(alb), fl(ulb), h0)
    out_l = ((hl_f + fl(hl_b)) * jax.nn.gelu(yl.astype(jnp.float32))).astype(xl.dtype)
    out_c = ((hc_f + fl(hc_b)) * jax.nn.gelu(yc.astype(jnp.float32))).astype(xc.dtype) if need_ctx else None
    return out_l, out_c


def _moe(h, w_router, b_router, w_gu, b_gu, w_down, b_down):
    n, d = h.shape
    logits = (h @ w_router + b_router).astype(jnp.float32)
    top_val, top_idx = lax.top_k(logits, TOP_K)
    gates = jax.nn.softmax(top_val, axis=-1)
    e_flat = top_idx.reshape(-1)
    tok_flat = jnp.repeat(jnp.arange(n), TOP_K)
    order = jnp.argsort(e_flat)
    e_s, tok_s, g_s = e_flat[order], tok_flat[order], gates.reshape(-1)[order]
    counts = jnp.bincount(e_flat, length=N_EXPERTS)
    padded = (counts + MOE_BLOCK - 1) // MOE_BLOCK * MOE_BLOCK
    start = jnp.cumsum(counts) - counts
    pend = jnp.cumsum(padded)
    dest = (pend - padded)[e_s] + (jnp.arange(n * TOP_K) - start[e_s])
    n_blocks = -(-(n * TOP_K + N_EXPERTS * (MOE_BLOCK - 1)) // MOE_BLOCK)
    rows = n_blocks * MOE_BLOCK
    row_tok = jnp.full((rows,), n, jnp.int32).at[dest].set(tok_s.astype(jnp.int32))
    row_gate = jnp.zeros((rows,), jnp.float32).at[dest].set(g_s)
    blk_exp = jnp.minimum(jnp.searchsorted(pend, jnp.arange(n_blocks) * MOE_BLOCK, side='right'), N_EXPERTS - 1)
    h_pad = jnp.concatenate([h, jnp.zeros((1, d), h.dtype)], axis=0)

    def run_block(acc, args):
        tok, gate, e = args
        xb = h_pad[tok]
        gu = xb @ w_gu[e] + b_gu[e]
        glu = jnp.minimum(gu[:, :D_EXPERT], SWIGLU_LIMIT)
        lin = jnp.clip(gu[:, D_EXPERT:], -SWIGLU_LIMIT, SWIGLU_LIMIT)
        act = (lin + 1.0) * glu * jax.nn.sigmoid(SWIGLU_ALPHA * glu)
        y = (act @ w_down[e] + b_down[e]) * gate[:, None].astype(h.dtype)
        return acc.at[tok].add(y.astype(acc.dtype)), None

    acc, _ = lax.scan(run_block, jnp.zeros((n + 1, d), h.dtype),
                      (row_tok.reshape(n_blocks, MOE_BLOCK), row_gate.reshape(n_blocks, MOE_BLOCK), blk_exp))
    return acc[:n]


def setup_inputs(seed: int = 0) -> dict:
    key = jax.random.key(seed)
    ks = jax.random.split(key, 32)
    f32 = jnp.float32
    L, D = DEPTH, D_MODEL

    def nrm(k, shape, s):
        return jax.random.normal(k, shape, f32) * s

    dt = jnp.exp(jax.random.uniform(ks[13], (L, 2, DN_HEADS), f32, math.log(1e-3), math.log(1e-1)))
    a_base = jax.random.uniform(ks[21], (L, 2, LRU_WIDTH), f32, 0.9, 0.999) ** (1.0 / LRU_C)
    return {
        'x': nrm(ks[0], (BATCH, SEQ, D), 1.0),
        'c': nrm(ks[1], (BATCH, D), 1.0),
        'ctx': nrm(ks[2], (BATCH, CTX_LEN, D), 1.0),
        'c_ctx': nrm(ks[3], (D,), 1.0),
        'w_mod': nrm(ks[4], (L, D, 6 * D), 0.5 * D ** -0.5),
        'b_mod': nrm(ks[5], (L, 6 * D), 0.02),
        'norm_mix_g': 1.0 + nrm(ks[6], (L, D), 0.02),
        'norm_ffn_g': 1.0 + nrm(ks[7], (L, D), 0.02),
        'w_in': nrm(ks[8], (L, D, IN_W), D ** -0.5),
        'w_out': nrm(ks[9], (L, MIX_W, D), MIX_W ** -0.5),
        'attn_sink': nrm(ks[10], (L, ATT_HEADS), 0.5),
        'dn_conv_w': nrm(ks[11], (L, CONV_W, 3 * DN_W), CONV_W ** -0.5),
        'dn_a_log': jnp.log(jax.random.uniform(ks[12], (L, 2, DN_HEADS), f32, 1.0, 16.0)),
        'dn_dt_bias': dt + jnp.log(-jnp.expm1(-dt)),
        'dn_norm_g': 1.0 + nrm(ks[14], (L, DN_HEAD_DIM), 0.02),
        'lru_conv_w': nrm(ks[15], (L, CONV_W, LRU_WIDTH), CONV_W ** -0.5),
        'lru_conv_b': nrm(ks[16], (L, LRU_WIDTH), 0.02),
        'lru_w_rgate': nrm(ks[17], (L, 2, LRU_BLOCKS, LRU_BLOCK_DIM, LRU_BLOCK_DIM), LRU_BLOCK_DIM ** -0.5),
        'lru_b_rgate': nrm(ks[18], (L, 2, LRU_WIDTH), 0.02),
        'lru_w_igate': nrm(ks[19], (L, 2, LRU_BLOCKS, LRU_BLOCK_DIM, LRU_BLOCK_DIM), LRU_BLOCK_DIM ** -0.5),
        'lru_b_igate': nrm(ks[20], (L, 2, LRU_WIDTH), 0.02),
        'lru_lambda': jnp.log(a_base) - jnp.log1p(-a_base),
        'w_router': nrm(ks[22], (L, D, N_EXPERTS), D ** -0.5),
        'b_router': nrm(ks[23], (L, N_EXPERTS), 0.01),
        'w_gu': nrm(ks[24], (L, N_EXPERTS, D, 2 * D_EXPERT), D ** -0.5),
        'b_gu': nrm(ks[25], (L, N_EXPERTS, 2 * D_EXPERT), 0.02),
        'w_down': nrm(ks[26], (L, N_EXPERTS, D_EXPERT, D), D_EXPERT ** -0.5),
        'b_down': nrm(ks[27], (L, N_EXPERTS, D), 0.02),
        'final_norm_g': 1.0 + nrm(ks[28], (D,), 0.02),
    }


def reference(x, c, ctx, c_ctx, w_mod, b_mod, norm_mix_g, norm_ffn_g, w_in, w_out, attn_sink,
              dn_conv_w, dn_a_log, dn_dt_bias, dn_norm_g, lru_conv_w, lru_conv_b,
              lru_w_rgate, lru_b_rgate, lru_w_igate, lru_b_igate, lru_lambda,
              w_router, b_router, w_gu, b_gu, w_down, b_down, final_norm_g):
    b, s, d = x.shape
    rows = s // GRID_W
    row = jnp.repeat(jnp.arange(rows), GRID_W)
    col = jnp.arange(rows * GRID_W) % GRID_W
    split_idx = np.cumsum(IN_SPLITS)[:-1].tolist()
    sc = jax.nn.silu(c)
    scc = jax.nn.silu(c_ctx)
    xl, xc = x, ctx
    for l in range(DEPTH):
        need_ctx = l < DEPTH - 1
        mod_l = jnp.split((sc @ w_mod[l] + b_mod[l])[:, None, :], 6, axis=-1)
        mod_c = jnp.split(scc @ w_mod[l] + b_mod[l], 6, axis=-1)
        hl = _modulate(xl, norm_mix_g[l], mod_l[0], mod_l[1])
        hc = _modulate(xc, norm_mix_g[l], mod_c[0], mod_c[1])
        pl = jnp.split(hl @ w_in[l], split_idx, axis=-1)
        pc = jnp.split(hc @ w_in[l], split_idx, axis=-1)
        att_l, att_c = _attention_mixer(pl[0:3], pc[0:3], attn_sink[l], row, col, need_ctx)
        dn_l, dn_c = _deltanet_mixer(pl[3:9], pc[3:9], dn_conv_w[l], dn_a_log[l], dn_dt_bias[l],
                                     dn_norm_g[l], need_ctx)
        lru_l, lru_c = _lru_mixer(pl[9:11], pc[9:11], lru_conv_w[l], lru_conv_b[l], lru_w_rgate[l],
                                  lru_b_rgate[l], lru_w_igate[l], lru_b_igate[l], lru_lambda[l], need_ctx)
        xl = xl + mod_l[2] * (jnp.concatenate([att_l, dn_l, lru_l], axis=-1) @ w_out[l])
        hl = _modulate(xl, norm_ffn_g[l], mod_l[3], mod_l[4])
        if need_ctx:
            xc = xc + mod_c[2] * (jnp.concatenate([att_c, dn_c, lru_c], axis=-1) @ w_out[l])
            hc = _modulate(xc, norm_ffn_g[l], mod_c[3], mod_c[4])
            y = _moe(jnp.concatenate([hl.reshape(-1, d), hc.reshape(-1, d)], axis=0),
                     w_router[l], b_router[l], w_gu[l], b_gu[l], w_down[l], b_down[l])
            xl = xl + mod_l[5] * y[:b * s].reshape(b, s, d)
            xc = xc + mod_c[5] * y[b * s:].reshape(xc.shape)
        else:
            y = _moe(hl.reshape(-1, d), w_router[l], b_router[l], w_gu[l], b_gu[l], w_down[l], b_down[l])
            xl = xl + mod_l[5] * y.reshape(b, s, d)
    return _rmsnorm(xl, final_norm_g)
```

```python
import functools
import math

import jax
import jax.numpy as jnp
from jax import lax
from jax.experimental import pallas as pl
from jax.experimental.pallas import tpu as pltpu

F32 = jnp.float32
BF16 = jnp.bfloat16
HIGHEST = lax.Precision.HIGHEST
SDS = jax.ShapeDtypeStruct

GRID_W = 64
ATT_HEADS = 16
ATT_KV_HEADS = 4
ATT_HEAD_DIM = 64
ATT_BLOCK = 128
ROPE_BASE = 10000.0
ATT_Q_W = ATT_HEADS * ATT_HEAD_DIM
ATT_KV_W = ATT_KV_HEADS * ATT_HEAD_DIM
DN_HEADS = 4
DN_HEAD_DIM = 128
DN_W = DN_HEADS * DN_HEAD_DIM
DN_CHUNK = 64
LRU_WIDTH = 512
LRU_C = 8.0
CONV_W = 4
CONV_LEFT = CONV_W // 2
N_EXPERTS = 32
TOP_K = 4
D_EXPERT = 1024
SWIGLU_LIMIT = 7.0
SWIGLU_ALPHA = 1.702
MOE_BLOCK = 256
EPS = 1e-6
NEG_INF = -1e30

COL_ATT_Q = 0
COL_ATT_K = ATT_Q_W
COL_ATT_V = COL_ATT_K + ATT_KV_W
COL_DN_QKV = COL_ATT_V + ATT_KV_W
COL_DN_GATE = COL_DN_QKV + 3 * DN_W
COL_LRU_X = COL_DN_GATE + DN_W
COL_LRU_Y = COL_LRU_X + LRU_WIDTH
MAIN_W = COL_LRU_Y + LRU_WIDTH
SMALL_W = 128
LANES = 128
HALO = 8
VMEM_LIMIT = 56 * 1024 * 1024


def _cparams(sem):
    return pltpu.CompilerParams(dimension_semantics=sem, vmem_limit_bytes=VMEM_LIMIT)


def _dot(a, b, precision=None):
    return jnp.dot(a, b, preferred_element_type=F32, precision=precision)


def _dot_nt(a, b, precision=None):
    return lax.dot_general(a, b, (((1,), (1,)), ((), ())), preferred_element_type=F32,
                           precision=precision)


def _softplus(x):
    return jnp.maximum(x, 0.0) + jnp.log1p(jnp.exp(-jnp.abs(x)))


def _rmsnorm_rows(x, g):
    return x * lax.rsqrt(jnp.mean(x * x, axis=-1, keepdims=True) + EPS) * g


def _mod_kernel(c_ref, w_ref, b_ref, o_ref):
    cs = c_ref[...]
    s = cs * jax.nn.sigmoid(cs)
    o_ref[0] = _dot(s, w_ref[0], HIGHEST) + b_ref[0]


def _modulation(c_all, w_mod, b_mod, tn=1024):
    depth, d, n = w_mod.shape
    return pl.pallas_call(
        _mod_kernel,
        grid=(depth, n // tn),
        in_specs=[pl.BlockSpec((8, d), lambda l, j: (0, 0)),
                  pl.BlockSpec((1, d, tn), lambda l, j: (l, 0, j)),
                  pl.BlockSpec((1, 1, tn), lambda l, j: (l, 0, j))],
        out_specs=pl.BlockSpec((1, 8, tn), lambda l, j: (l, 0, j)),
        out_shape=SDS((depth, 8, n), F32),
        compiler_params=_cparams(("parallel", "parallel")),
        name="modulation",
    )(c_all, w_mod, b_mod.reshape(depth, 1, n))


def _mod_row(i, tm, nl, s, b):
    return jnp.where(i * tm < nl, (i * tm) // s, b)


def _in_proj_kernel(x_ref, mod_ref, g_ref, w_ref, ws_ref, o_ref, os_ref, xn_ref):
    @pl.when(pl.program_id(1) == 0)
    def _():
        hn = _rmsnorm_rows(x_ref[...], g_ref[...]) * (1.0 + mod_ref[1:2, :]) + mod_ref[0:1, :]
        hb = hn.astype(BF16)
        xn_ref[...] = hb
        os_ref[...] = _dot(hb, ws_ref[...])

    o_ref[...] = _dot(xn_ref[...], w_ref[...])


def _in_proj(x, mod, g, w_main, w_small, layer, *, batch, seq, tn=768):
    m, d = x.shape
    n = w_main.shape[1]
    nl = batch * seq
    tm = math.gcd(math.gcd(nl, m - nl), 1024)
    row = functools.partial(_mod_row, tm=tm, nl=nl, s=seq, b=batch)
    return pl.pallas_call(
        _in_proj_kernel,
        grid=(m // tm, n // tn),
        in_specs=[pl.BlockSpec((tm, d), lambda i, j: (i, 0)),
                  pl.BlockSpec((None, None, 6, d), lambda i, j: (layer, row(i), 0, 0)),
                  pl.BlockSpec((1, d), lambda i, j: (0, 0)),
                  pl.BlockSpec((d, tn), lambda i, j: (0, j)),
                  pl.BlockSpec((d, SMALL_W), lambda i, j: (0, 0))],
        out_specs=[pl.BlockSpec((tm, tn), lambda i, j: (i, j)),
                   pl.BlockSpec((tm, SMALL_W), lambda i, j: (i, 0))],
        out_shape=[SDS((m, n), F32), SDS((m, SMALL_W), F32)],
        scratch_shapes=[pltpu.VMEM((tm, d), BF16)],
        compiler_params=_cparams(("parallel", "arbitrary")),
        name="in_proj",
    )(x, mod, g, w_main, w_small)


def _rope_tables(seq):
    t = jnp.arange(seq)
    half = ATT_HEAD_DIM // 2
    inv = ROPE_BASE ** (-jnp.arange(0, half, 2, dtype=F32) / half)
    ang_r = (t // GRID_W).astype(F32)[:, None] * inv
    ang_c = (t % GRID_W).astype(F32)[:, None] * inv
    cr, sr, cc, sc = jnp.cos(ang_r), jnp.sin(ang_r), jnp.cos(ang_c), jnp.sin(ang_c)
    cos = jnp.concatenate([cr, cr, cc, cc], axis=-1)
    sin = jnp.concatenate([-sr, sr, -sc, sc], axis=-1)
    cos = jnp.concatenate([jnp.tile(cos, (1, 2)), jnp.ones((ATT_BLOCK, LANES), F32)], axis=0)
    sin = jnp.concatenate([jnp.tile(sin, (1, 2)), jnp.zeros((ATT_BLOCK, LANES), F32)], axis=0)
    return cos, sin


def _rope(x, cos, sin):
    w = x.shape[1]
    rep = w // LANES
    c = jnp.tile(cos, (1, rep)) if rep > 1 else cos
    s = jnp.tile(sin, (1, rep)) if rep > 1 else sin
    lane = lax.broadcasted_iota(jnp.int32, x.shape, 1)
    quarter = ATT_HEAD_DIM // 4
    first = (lane % (2 * quarter)) < quarter
    swapped = jnp.where(first, pltpu.roll(x, w - quarter, 1), pltpu.roll(x, quarter, 1))
    return x * c + swapped * s


def _attn_kernel(sink_ref, q_ref, kp_ref, kc_ref, kn_ref, kx_ref, vp_ref, vc_ref, vn_ref, vx_ref,
                 cq_ref, sq_ref, cp_ref, sp_ref, cc_ref, sc_ref, cn_ref, sn_ref, o_ref,
                 *, n_lat_blk, blk_per_seq):
    qb = pl.program_id(0)
    is_lat = qb < n_lat_blk
    i = qb % blk_per_seq
    blk = ATT_BLOCK
    scale = ATT_HEAD_DIM ** -0.5

    q = _rope(q_ref[...], cq_ref[...], sq_ref[...]) * scale
    kp = _rope(kp_ref[...], cp_ref[...], sp_ref[...])
    kc = _rope(kc_ref[...], cc_ref[...], sc_ref[...])
    kn = _rope(kn_ref[...], cn_ref[...], sn_ref[...])
    kall = jnp.concatenate([kp, kc, kn, kx_ref[...]], axis=0)
    vall = jnp.concatenate([vp_ref[...], vc_ref[...], vn_ref[...], vx_ref[...]], axis=0)
    nkeys = kall.shape[0]

    off = 2 * blk
    r = lax.broadcasted_iota(jnp.int32, (blk, blk), 0)
    n = lax.broadcasted_iota(jnp.int32, (blk, blk), 1)
    m_prev = n >= r + jnp.where(is_lat & (i >= 1), 0, off)
    m_cur = n >= jnp.where(is_lat, 0, off)
    m_next = n <= r - jnp.where(is_lat & (i < blk_per_seq - 1), 0, off)
    m_ctx = jnp.full((blk, nkeys - 3 * blk), True)
    mask = jnp.concatenate([m_prev, m_cur, m_next, m_ctx], axis=1)

    lane = lax.broadcasted_iota(jnp.int32, (nkeys, LANES), 1)
    low = lane < ATT_HEAD_DIM
    for hk in range(ATT_KV_HEADS):
        grp = (hk // 2) * LANES
        kg = kall[:, grp:grp + LANES]
        vg = vall[:, grp:grp + LANES]
        kr = pltpu.roll(kg, ATT_HEAD_DIM, 1)
        vr = pltpu.roll(vg, ATT_HEAD_DIM, 1)
        if hk % 2 == 0:
            k_lo, k_hi, v_lo, v_hi = kg, kr, vg, vr
        else:
            k_lo, k_hi, v_lo, v_hi = kr, kg, vr, vg
        halves = ((jnp.where(low, k_lo, 0.0).astype(BF16), jnp.where(low, v_lo, 0.0).astype(BF16)),
                  (jnp.where(low, 0.0, k_hi).astype(BF16), jnp.where(low, 0.0, v_hi).astype(BF16)))
        for p in range(2):
            col = hk * 2 * LANES + p * LANES
            qp = q[:, col:col + LANES].astype(BF16)
            acc = jnp.zeros((blk, LANES), F32)
            for half in range(2):
                h = hk * 4 + p * 2 + half
                kh, vh = halves[half]
                sink = sink_ref[h]
                lg = jnp.where(mask, _dot_nt(qp, kh), NEG_INF)
                mx = jnp.maximum(jnp.max(lg, axis=-1, keepdims=True), sink)
                pe = jnp.exp(lg - mx)
                den = jnp.sum(pe, axis=-1, keepdims=True) + jnp.exp(sink - mx)
                acc = acc + _dot(pe.astype(BF16), vh) / den
            o_ref[:, col:col + LANES] = acc.astype(o_ref.dtype)


def _attention(p, sink, cos, sin, *, batch, seq, ctx_len, with_ctx):
    blk = ATT_BLOCK
    nl = batch * seq
    n_lat_blk = nl // blk
    bps = seq // blk
    cps = ctx_len // blk
    nq = n_lat_blk + (batch * cps if with_ctx else 0)
    kcol, vcol = COL_ATT_K // ATT_KV_W, COL_ATT_V // ATT_KV_W

    def parts(qb):
        is_lat = qb < n_lat_blk
        b = jnp.where(is_lat, qb // bps, (qb - n_lat_blk) // cps)
        i = jnp.where(is_lat, qb % bps, 0)
        return is_lat, b, i

    def kblk(qb, off):
        is_lat, b, i = parts(qb)
        return jnp.where(is_lat, b * bps + jnp.clip(i + off, 0, bps - 1), 0)

    def posblk(qb, off):
        is_lat, _, i = parts(qb)
        return jnp.where(is_lat, jnp.clip(i + off, 0, bps - 1), bps)

    def ctxblk(qb):
        return nl // ctx_len + parts(qb)[1]

    kv = lambda off, col: pl.BlockSpec((blk, ATT_KV_W), lambda qb: (kblk(qb, off), col))
    cx = lambda col: pl.BlockSpec((ctx_len, ATT_KV_W), lambda qb: (ctxblk(qb), col))
    tab = lambda off: pl.BlockSpec((blk, LANES), lambda qb: (posblk(qb, off), 0))
    in_specs = [pl.BlockSpec(memory_space=pltpu.SMEM),
                pl.BlockSpec((blk, ATT_Q_W), lambda qb: (qb, 0)),
                kv(-1, kcol), kv(0, kcol), kv(1, kcol), cx(kcol),
                kv(-1, vcol), kv(0, vcol), kv(1, vcol), cx(vcol),
                tab(0), tab(0), tab(-1), tab(-1), tab(0), tab(0), tab(1), tab(1)]
    return pl.pallas_call(
        functools.partial(_attn_kernel, n_lat_blk=n_lat_blk, blk_per_seq=bps),
        grid=(nq,),
        in_specs=in_specs,
        out_specs=pl.BlockSpec((blk, ATT_Q_W), lambda qb: (qb, 0)),
        out_shape=SDS((nq * blk, ATT_Q_W), BF16),
        compiler_params=_cparams(("parallel",)),
        name="attention",
    )(sink, p, p, p, p, p, p, p, p, p, cos, sin, cos, sin, cos, sin, cos, sin)


def _conv_tile(x_ref, prev_ref, next_ref, w_ref, buf_ref, first, last):
    tm = x_ref.shape[0]
    buf_ref[0:HALO, :] = prev_ref[...] * jnp.where(first, 0.0, 1.0)
    buf_ref[HALO:HALO + tm, :] = x_ref[...]
    buf_ref[HALO + tm:2 * HALO + tm, :] = next_ref[...] * jnp.where(last, 0.0, 1.0)
    out = None
    for j in range(CONV_W):
        start = HALO - CONV_LEFT + j
        term = buf_ref[start:start + tm, :] * w_ref[j:j + 1, :]
        out = term if out is None else out + term
    return out


def _lru_kernel(x_ref, prev_ref, next_ref, cw_ref, cb_ref, wr_ref, br_ref, wi_ref, bi_ref,
                lam_ref, o_ref, buf_ref, carry_ref, *, ctx_tiles, lat_tiles, reverse):
    t = pl.program_id(1)
    tm = x_ref.shape[0]
    in_ctx = t < ctx_tiles
    j = jnp.where(in_ctx, t, t - ctx_tiles)
    nseg = jnp.where(in_ctx, ctx_tiles, lat_tiles)
    jj = nseg - 1 - j if reverse else j
    xc = _conv_tile(x_ref, prev_ref, next_ref, cw_ref, buf_ref, jj == 0, jj == nseg - 1)
    xc = xc + cb_ref[...]
    xb = xc.astype(BF16)
    r = jax.nn.sigmoid(_dot(xb, wr_ref[...]) + br_ref[...])
    gi = jax.nn.sigmoid(_dot(xb, wi_ref[...]) + bi_ref[...])
    log_sig = -_softplus(-lam_ref[...])
    log_a = LRU_C * r * log_sig
    a = jnp.exp(log_a)
    u = jnp.sqrt(1.0 - jnp.exp(2.0 * log_a)) * (gi * xc)

    row = lax.broadcasted_iota(jnp.int32, (tm, 1), 0)
    s = 1
    while s < tm:
        shift = tm - s if reverse else s
        a_sh, u_sh = pltpu.roll(a, shift, 0), pltpu.roll(u, shift, 0)
        valid = (row < tm - s) if reverse else (row >= s)
        u = jnp.where(valid, a * u_sh + u, u)
        a = jnp.where(valid, a * a_sh, a)
        s *= 2

    @pl.when(t == 0)
    def _():
        carry_ref[...] = jnp.zeros_like(carry_ref)

    h = a * carry_ref[0:1, :] + u
    o_ref[...] = h
    carry_ref[0:1, :] = h[0:1, :] if reverse else h[tm - 1:tm, :]


def _lru(p, conv_w, conv_b, w_r, b_r, w_i, b_i, lam, *, batch, seq, ctx_len, reverse, tm=256):
    m = p.shape[0]
    nl = batch * seq
    lat_tiles, ctx_tiles = seq // tm, ctx_len // tm
    steps = lat_tiles + ctx_tiles
    xcol = COL_LRU_X // LRU_WIDTH
    hpt = tm // HALO
    nh = m // HALO

    def tile(b, t):
        in_ctx = t < ctx_tiles
        j = jnp.where(in_ctx, t, t - ctx_tiles)
        nseg = jnp.where(in_ctx, ctx_tiles, lat_tiles)
        jj = nseg - 1 - j if reverse else j
        return jnp.where(in_ctx, nl // tm + b * ctx_tiles + jj, b * lat_tiles + jj)

    const = lambda shape: pl.BlockSpec(shape, lambda b, t: (0,) * len(shape))
    return pl.pallas_call(
        functools.partial(_lru_kernel, ctx_tiles=ctx_tiles, lat_tiles=lat_tiles, reverse=reverse),
        grid=(batch, steps),
        in_specs=[pl.BlockSpec((tm, LRU_WIDTH), lambda b, t: (tile(b, t), xcol)),
                  pl.BlockSpec((HALO, LRU_WIDTH),
                               lambda b, t: (jnp.maximum(tile(b, t) * hpt - 1, 0), xcol)),
                  pl.BlockSpec((HALO, LRU_WIDTH),
                               lambda b, t: (jnp.minimum((tile(b, t) + 1) * hpt, nh - 1), xcol)),
                  const((CONV_W, LRU_WIDTH)), const((1, LRU_WIDTH)),
                  const((LRU_WIDTH, LRU_WIDTH)), const((1, LRU_WIDTH)),
                  const((LRU_WIDTH, LRU_WIDTH)), const((1, LRU_WIDTH)),
                  const((1, LRU_WIDTH))],
        out_specs=pl.BlockSpec((tm, LRU_WIDTH), lambda b, t: (tile(b, t), 0)),
        out_shape=SDS((m, LRU_WIDTH), F32),
        scratch_shapes=[pltpu.VMEM((tm + 2 * HALO, LRU_WIDTH), F32),
                        pltpu.VMEM((8, LRU_WIDTH), F32)],
        compiler_params=_cparams(("parallel", "arbitrary")),
        name="rglru_bwd" if reverse else "rglru_fwd",
    )(p, p, p, conv_w, conv_b, w_r, b_r, w_i, b_i, lam)


def _block_diag(w):
    nb, k, _ = w.shape
    eye = jnp.eye(nb, dtype=w.dtype)
    return jnp.einsum('nij,nm->nimj', w, eye).reshape(nb * k, nb * k)


def _dn_prep_kernel(x_ref, prev_ref, next_ref, ps_ref, cw_ref, alog_c_ref, dt_c_ref, alog_r_ref,
                    dt_r_ref, u_ref, w_ref, qe_ref, kdt_ref, qk_ref, eg_ref, buf_ref, qkv_ref,
                    *, nl, lat_tiles, ctx_tiles):
    tm = x_ref.shape[0]
    c_sz = DN_CHUNK
    i = pl.program_id(0)
    is_lat = i * tm < nl
    j = jnp.where(is_lat, i % lat_tiles, (i - nl // tm) % ctx_tiles)
    nseg = jnp.where(is_lat, lat_tiles, ctx_tiles)
    act = _conv_tile(x_ref, prev_ref, next_ref, cw_ref, buf_ref, j == 0, j == nseg - 1)
    act = act * jax.nn.sigmoid(act)
    for h in range(DN_HEADS):
        lo = h * DN_HEAD_DIM
        qh = act[:, lo:lo + DN_HEAD_DIM]
        kh = act[:, DN_W + lo:DN_W + lo + DN_HEAD_DIM]
        qkv_ref[:, lo:lo + DN_HEAD_DIM] = (
            qh * lax.rsqrt(jnp.sum(qh * qh, axis=-1, keepdims=True) + EPS) * DN_HEAD_DIM ** -0.5)
        qkv_ref[:, DN_W + lo:DN_W + lo + DN_HEAD_DIM] = (
            kh * lax.rsqrt(jnp.sum(kh * kh, axis=-1, keepdims=True) + EPS))
    qkv_ref[:, 2 * DN_W:3 * DN_W] = act[:, 2 * DN_W:3 * DN_W]

    nhd = 2 * DN_HEADS
    ps = ps_ref[...]
    beta_c = jax.nn.sigmoid(ps[:, 0:nhd])
    g_c = -jnp.exp(alog_c_ref[...]) * _softplus(ps[:, nhd:2 * nhd] + dt_c_ref[...])
    eye_l = (lax.broadcasted_iota(jnp.int32, (LANES, LANES), 0)
             == lax.broadcasted_iota(jnp.int32, (LANES, LANES), 1))
    ps_t = _dot_nt(eye_l.astype(F32), ps, HIGHEST)
    g_r = -jnp.exp(alog_r_ref[...]) * _softplus(ps_t[nhd:2 * nhd, :] + dt_r_ref[...])

    ri = lax.broadcasted_iota(jnp.int32, (c_sz, c_sz), 0)
    ci = lax.broadcasted_iota(jnp.int32, (c_sz, c_sz), 1)
    lower_incl = ci <= ri
    upper_incl = ci >= ri
    tri_l = lower_incl.astype(F32)
    tri_u = upper_incl.astype(F32)
    eye_c = (ri == ci).astype(F32)
    col8 = lax.broadcasted_iota(jnp.int32, (c_sz, nhd), 1)
    row8 = lax.broadcasted_iota(jnp.int32, (nhd, c_sz), 0)
    eye_b = eye_l.astype(BF16)

    for c in range(tm // c_sz):
        rows = slice(c * c_sz, (c + 1) * c_sz)
        gcc = g_c[rows, :]
        gcum_c = jnp.where(col8 < DN_HEADS, _dot(tri_l, gcc, HIGHEST), _dot(tri_u, gcc, HIGHEST))
        grr = g_r[:, rows]
        gcum_r = jnp.where(row8 < DN_HEADS, _dot(grr, tri_u, HIGHEST), _dot(grr, tri_l, HIGHEST))
        for d in range(2):
            incl = lower_incl if d == 0 else upper_incl
            strict = (ci < ri) if d == 0 else (ci > ri)
            end_row = c_sz - 1 if d == 0 else 0
            for h in range(DN_HEADS):
                idx = d * DN_HEADS + h
                lo = h * DN_HEAD_DIM
                qh = qkv_ref[rows, lo:lo + DN_HEAD_DIM]
                kh = qkv_ref[rows, DN_W + lo:DN_W + lo + DN_HEAD_DIM]
                vh = qkv_ref[rows, 2 * DN_W + lo:2 * DN_W + lo + DN_HEAD_DIM]
                beta = beta_c[rows, idx:idx + 1]
                gc = gcum_c[:, idx:idx + 1]
                gr = gcum_r[idx:idx + 1, :]
                gtot = gc[end_row:end_row + 1, :]
                dec = jnp.exp(jnp.where(incl, gc - gr, -jnp.inf))
                kb = kh * beta
                khb = kh.astype(BF16)
                a_mat = jnp.where(strict, _dot_nt(kb.astype(BF16), khb) * dec, 0.0)
                pw = -a_mat
                t_inv = eye_c + pw
                for _ in range(5):
                    pw = _dot(pw, pw, HIGHEST)
                    t_inv = t_inv + _dot(t_inv, pw, HIGHEST)
                eg = jnp.exp(gc)
                rhs = jnp.concatenate([vh * beta, kb * eg], axis=1)
                sol = _dot(t_inv, rhs, HIGHEST)
                qk = _dot_nt(qh.astype(BF16), khb) * dec
                kd = (kh * jnp.exp(gtot - gc)).astype(BF16)
                u_ref[d, rows, lo:lo + DN_HEAD_DIM] = sol[:, :DN_HEAD_DIM]
                w_ref[d, rows, lo:lo + DN_HEAD_DIM] = sol[:, DN_HEAD_DIM:].astype(BF16)
                qe_ref[d, rows, lo:lo + DN_HEAD_DIM] = (qh * eg).astype(BF16)
                kdt_ref[d, c, h] = _dot_nt(eye_b, kd).astype(BF16)
                qk_ref[d, c, h] = qk.astype(BF16)
                eg_ref[c, idx:idx + 1, :] = jnp.broadcast_to(jnp.exp(gtot), (1, LANES))


def _dn_prep(p, ps, conv_w, a_log, dt_bias, *, batch, seq, ctx_len, tm=128):
    m = p.shape[0]
    nl = batch * seq
    cpt = tm // DN_CHUNK
    nchunks = m // DN_CHUNK
    hpt = tm // HALO
    nh = m // HALO
    qcol = COL_DN_QKV // (3 * DN_W)
    nhd = 2 * DN_HEADS
    const = lambda shape: pl.BlockSpec(shape, lambda i: (0,) * len(shape))
    return pl.pallas_call(
        functools.partial(_dn_prep_kernel, nl=nl, lat_tiles=seq // tm, ctx_tiles=ctx_len // tm),
        grid=(m // tm,),
        in_specs=[pl.BlockSpec((tm, 3 * DN_W), lambda i: (i, qcol)),
                  pl.BlockSpec((HALO, 3 * DN_W), lambda i: (jnp.maximum(i * hpt - 1, 0), qcol)),
                  pl.BlockSpec((HALO, 3 * DN_W), lambda i: (jnp.minimum((i + 1) * hpt, nh - 1), qcol)),
                  pl.BlockSpec((tm, SMALL_W), lambda i: (i, 0)),
                  const((CONV_W, 3 * DN_W)),
                  const((1, nhd)), const((1, nhd)), const((nhd, 1)), const((nhd, 1))],
        out_specs=[pl.BlockSpec((2, tm, DN_W), lambda i: (0, i, 0)),
                   pl.BlockSpec((2, tm, DN_W), lambda i: (0, i, 0)),
                   pl.BlockSpec((2, tm, DN_W), lambda i: (0, i, 0)),
                   pl.BlockSpec((2, cpt, DN_HEADS, DN_HEAD_DIM, DN_CHUNK), lambda i: (0, i, 0, 0, 0)),
                   pl.BlockSpec((2, cpt, DN_HEADS, DN_CHUNK, DN_CHUNK), lambda i: (0, i, 0, 0, 0)),
                   pl.BlockSpec((cpt, 8, LANES), lambda i: (i, 0, 0))],
        out_shape=[SDS((2, m, DN_W), F32), SDS((2, m, DN_W), BF16), SDS((2, m, DN_W), BF16),
                   SDS((2, nchunks, DN_HEADS, DN_HEAD_DIM, DN_CHUNK), BF16),
                   SDS((2, nchunks, DN_HEADS, DN_CHUNK, DN_CHUNK), BF16),
                   SDS((nchunks, 8, LANES), F32)],
        scratch_shapes=[pltpu.VMEM((tm + 2 * HALO, 3 * DN_W), F32),
                        pltpu.VMEM((tm, 3 * DN_W), F32)],
        compiler_params=_cparams(("parallel",)),
        name="deltanet_prep",
    )(p, p, p, ps, conv_w, a_log.reshape(1, nhd), dt_bias.reshape(1, nhd),
      a_log.reshape(nhd, 1), dt_bias.reshape(nhd, 1))


def _dn_scan_kernel(uf_ref, wf_ref, qef_ref, kdtf_ref, qkf_ref, egf_ref,
                    ub_ref, wb_ref, qeb_ref, kdtb_ref, qkb_ref, egb_ref, of_ref, ob_ref, s_ref):
    @pl.when(pl.program_id(1) == 0)
    def _():
        s_ref[...] = jnp.zeros_like(s_ref)

    dirs = ((uf_ref, wf_ref, qef_ref, kdtf_ref, qkf_ref, egf_ref, of_ref),
            (ub_ref, wb_ref, qeb_ref, kdtb_ref, qkb_ref, egb_ref, ob_ref))
    for d, (u_ref, w_ref, qe_ref, kdt_ref, qk_ref, eg_ref, o_ref) in enumerate(dirs):
        for h in range(DN_HEADS):
            idx = d * DN_HEADS + h
            lo = h * DN_HEAD_DIM
            st = s_ref[idx]
            sb = st.astype(BF16)
            v_new = u_ref[:, lo:lo + DN_HEAD_DIM] - _dot(w_ref[:, lo:lo + DN_HEAD_DIM], sb)
            vb = v_new.astype(BF16)
            o_ref[:, lo:lo + DN_HEAD_DIM] = (
                _dot(qe_ref[:, lo:lo + DN_HEAD_DIM], sb) + _dot(qk_ref[h], vb))
            s_ref[idx] = st * eg_ref[idx:idx + 1, :] + _dot(kdt_ref[h], vb)


def _dn_scan(u, w, qe, kdt, qk, eg, *, batch, seq, ctx_len):
    m = u.shape[1]
    c_sz = DN_CHUNK
    nl = batch * seq
    lat_ch, ctx_ch = seq // c_sz, ctx_len // c_sz
    steps = lat_ch + ctx_ch

    def chunk(d, b, s):
        in_ctx = s < ctx_ch
        j = jnp.where(in_ctx, s, s - ctx_ch)
        nseg = jnp.where(in_ctx, ctx_ch, lat_ch)
        jj = j if d == 0 else nseg - 1 - j
        return jnp.where(in_ctx, nl // c_sz + b * ctx_ch + jj, b * lat_ch + jj)

    def specs(d):
        rows = lambda width: pl.BlockSpec((None, c_sz, width), lambda b, s: (d, chunk(d, b, s), 0))
        return [rows(DN_W), rows(DN_W), rows(DN_W),
                pl.BlockSpec((None, None, DN_HEADS, DN_HEAD_DIM, c_sz),
                             lambda b, s: (d, chunk(d, b, s), 0, 0, 0)),
                pl.BlockSpec((None, None, DN_HEADS, c_sz, c_sz),
                             lambda b, s: (d, chunk(d, b, s), 0, 0, 0)),
                pl.BlockSpec((None, 8, LANES), lambda b, s: (chunk(d, b, s), 0, 0))]

    out_spec = lambda d: pl.BlockSpec((c_sz, DN_W), lambda b, s: (chunk(d, b, s), 0))
    return pl.pallas_call(
        _dn_scan_kernel,
        grid=(batch, steps),
        in_specs=specs(0) + specs(1),
        out_specs=[out_spec(0), out_spec(1)],
        out_shape=[SDS((m, DN_W), F32), SDS((m, DN_W), F32)],
        scratch_shapes=[pltpu.VMEM((2 * DN_HEADS, DN_HEAD_DIM, DN_HEAD_DIM), F32)],
        compiler_params=_cparams(("parallel", "arbitrary")),
        name="deltanet_scan",
    )(u, w, qe, kdt, qk, eg, u, w, qe, kdt, qk, eg)


def _out_proj_kernel(att_ref, dnf_ref, dnb_ref, dng_ref, lf_ref, lb_ref, ly_ref, x_ref, w_ref,
                     mod_ref, gffn_ref, dn_g_ref, wr_ref, br_ref, xo_ref, h_ref, lg_ref):
    o = dnf_ref[...] + dnb_ref[...]
    gate = dng_ref[...]
    dn_parts = []
    for h in range(DN_HEADS):
        lo = h * DN_HEAD_DIM
        oh = o[:, lo:lo + DN_HEAD_DIM]
        gh = gate[:, lo:lo + DN_HEAD_DIM]
        y = oh * lax.rsqrt(jnp.mean(oh * oh, axis=-1, keepdims=True) + EPS) * dn_g_ref[...]
        dn_parts.append((y * (gh * jax.nn.sigmoid(gh))).astype(BF16))
    dn = jnp.concatenate(dn_parts, axis=1)
    lru = ((lf_ref[...] + lb_ref[...]) * jax.nn.gelu(ly_ref[...])).astype(BF16)
    acc = (_dot(att_ref[...], w_ref[0:ATT_Q_W, :])
           + _dot(dn, w_ref[ATT_Q_W:ATT_Q_W + DN_W, :])
           + _dot(lru, w_ref[ATT_Q_W + DN_W:, :]))
    xn = x_ref[...] + mod_ref[2:3, :] * acc
    xo_ref[...] = xn
    hn = _rmsnorm_rows(xn, gffn_ref[...]) * (1.0 + mod_ref[4:5, :]) + mod_ref[3:4, :]
    h_ref[...] = hn
    lg_ref[...] = _dot(hn, wr_ref[...], HIGHEST) + br_ref[...]


def _out_proj(att, dn_f, dn_b, p, lru_f, lru_b, x, w_out, mod, g_ffn, dn_g, w_router, b_router,
              layer, *, batch, seq, rows, tm=256):
    d = x.shape[1]
    nl = batch * seq
    row = functools.partial(_mod_row, tm=tm, nl=nl, s=seq, b=batch)
    gcol, ycol = COL_DN_GATE // DN_W, COL_LRU_Y // LRU_WIDTH
    const = lambda shape: pl.BlockSpec(shape, lambda i: (0,) * len(shape))
    tile = lambda width, col=0: pl.BlockSpec((tm, width), lambda i: (i, col))
    return pl.pallas_call(
        _out_proj_kernel,
        grid=(rows // tm,),
        in_specs=[tile(ATT_Q_W), tile(DN_W), tile(DN_W), tile(DN_W, gcol),
                  tile(LRU_WIDTH), tile(LRU_WIDTH), tile(LRU_WIDTH, ycol), tile(d),
                  const((d, d)),
                  pl.BlockSpec((None, None, 6, d), lambda i: (layer, row(i), 0, 0)),
                  const((1, d)), const((1, DN_HEAD_DIM)), const((d, LANES)), const((1, LANES))],
        out_specs=[tile(d), tile(d), tile(LANES)],
        out_shape=[SDS((rows, d), F32), SDS((rows, d), F32), SDS((rows, LANES), F32)],
        compiler_params=_cparams(("parallel",)),
        name="out_proj",
    )(att, dn_f, dn_b, p, lru_f, lru_b, p, x, w_out, mod, g_ffn, dn_g, w_router, b_router)


ROW_TOK_BITS = 15


def _moe_kernel(exp_ref, row_ref, nused_ref, h_hbm, wgu_ref, bgu_ref, wd_ref, bd_ref, gate_ref,
                o_hbm, xbuf, ybuf, gsem, ssem, *, n_tok):
    del exp_ref
    i = pl.program_id(0)
    nb = pl.num_programs(0)
    nused = nused_ref[0]
    slot = i % 2
    rows = MOE_BLOCK
    tok_mask = (1 << ROW_TOK_BITS) - 1

    def gather(blk, sl):
        def body(r, carry):
            tok = row_ref[blk * rows + r] & tok_mask
            pltpu.make_async_copy(h_hbm.at[pl.ds(tok, 1)], xbuf.at[sl, pl.ds(r, 1)],
                                  gsem.at[sl]).start()
            return carry
        lax.fori_loop(0, rows, body, 0)

    def scatter(blk, sl):
        def body(r, carry):
            word = row_ref[blk * rows + r]
            tok = word & tok_mask
            k = (word >> ROW_TOK_BITS) & (TOP_K - 1)
            valid = (word >> (ROW_TOK_BITS + 2)) & 1
            dst = jnp.where(valid == 1, tok, n_tok + r)
            pltpu.make_async_copy(ybuf.at[sl, pl.ds(r, 1)], o_hbm.at[k, pl.ds(dst, 1)],
                                  ssem.at[sl]).start()
            return carry
        lax.fori_loop(0, rows, body, 0)

    def wait_gather(sl):
        pltpu.make_async_copy(h_hbm.at[pl.ds(0, rows)], xbuf.at[sl], gsem.at[sl]).wait()

    def wait_scatter(sl):
        pltpu.make_async_copy(ybuf.at[sl], o_hbm.at[0, pl.ds(0, rows)], ssem.at[sl]).wait()

    @pl.when(i == 0)
    def _():
        gather(0, 0)
        ybuf[1] = jnp.zeros((rows, ybuf.shape[2]), F32)
        spare = [pltpu.make_async_copy(ybuf.at[1], o_hbm.at[k, pl.ds(n_tok, rows)], ssem.at[1])
                 for k in range(TOP_K)]
        for cp in spare:
            cp.start()
        for cp in spare:
            cp.wait()

    @pl.when(i + 1 < nused)
    def _():
        gather(i + 1, 1 - slot)

    @pl.when(i < nused)
    def _():
        wait_gather(slot)

        @pl.when(i >= 2)
        def _():
            wait_scatter(slot)

        xb = xbuf[slot].astype(BF16)
        gu = _dot(xb, wgu_ref[0]) + bgu_ref[0]
        glu = jnp.minimum(gu[:, :D_EXPERT], SWIGLU_LIMIT)
        lin = jnp.clip(gu[:, D_EXPERT:], -SWIGLU_LIMIT, SWIGLU_LIMIT)
        act = (lin + 1.0) * glu * jax.nn.sigmoid(SWIGLU_ALPHA * glu)
        y = (_dot(act.astype(BF16), wd_ref[0]) + bd_ref[0]) * gate_ref[...]
        ybuf[slot] = y
        scatter(i, slot)

    @pl.when(i == nb - 1)
    def _():
        @pl.when(nused >= 2)
        def _():
            wait_scatter(nused % 2)

        wait_scatter((nused - 1) % 2)


def _moe(h, blk_exp, row_word, n_used, row_gate, w_gu, b_gu, w_down, b_down):
    n_tok, d = h.shape
    n_blocks = blk_exp.shape[0]
    wmap = lambda i, e, r, u: (e[i], 0, 0)
    grid_spec = pltpu.PrefetchScalarGridSpec(
        num_scalar_prefetch=3,
        grid=(n_blocks,),
        in_specs=[pl.BlockSpec(memory_space=pl.ANY),
                  pl.BlockSpec((1, d, 2 * D_EXPERT), wmap),
                  pl.BlockSpec((1, 1, 2 * D_EXPERT), wmap),
                  pl.BlockSpec((1, D_EXPERT, d), wmap),
                  pl.BlockSpec((1, 1, d), wmap),
                  pl.BlockSpec((MOE_BLOCK, 1), lambda i, e, r, u: (i, 0))],
        out_specs=pl.BlockSpec(memory_space=pl.ANY),
        scratch_shapes=[pltpu.VMEM((2, MOE_BLOCK, d), F32),
                        pltpu.VMEM((2, MOE_BLOCK, d), F32),
                        pltpu.SemaphoreType.DMA((2,)),
                        pltpu.SemaphoreType.DMA((2,))])
    return pl.pallas_call(
        functools.partial(_moe_kernel, n_tok=n_tok),
        grid_spec=grid_spec,
        out_shape=SDS((TOP_K, n_tok + MOE_BLOCK, d), F32),
        compiler_params=_cparams(("arbitrary",)),
        name="moe_experts",
    )(blk_exp, row_word, n_used, h, w_gu, b_gu.reshape(N_EXPERTS, 1, -1), w_down,
      b_down.reshape(N_EXPERTS, 1, -1), row_gate)


def _route(logits):
    n = logits.shape[0]
    top_val, top_idx = lax.top_k(logits, TOP_K)
    gates = jax.nn.softmax(top_val, axis=-1)
    e_flat = top_idx.reshape(-1)
    order = jnp.argsort(e_flat)
    e_s = e_flat[order]
    counts = jnp.bincount(e_flat, length=N_EXPERTS)
    padded = (counts + MOE_BLOCK - 1) // MOE_BLOCK * MOE_BLOCK
    start = jnp.cumsum(counts) - counts
    pend = jnp.cumsum(padded)
    dest = (pend - padded)[e_s] + (jnp.arange(n * TOP_K) - start[e_s])
    n_blocks = -(-(n * TOP_K + N_EXPERTS * (MOE_BLOCK - 1)) // MOE_BLOCK)
    rows = n_blocks * MOE_BLOCK
    word = (order // TOP_K) | ((order % TOP_K) << ROW_TOK_BITS) | (1 << (ROW_TOK_BITS + 2))
    row_word = jnp.zeros((rows,), jnp.int32).at[dest].set(word.astype(jnp.int32))
    row_gate = jnp.zeros((rows,), F32).at[dest].set(gates.reshape(-1)[order])
    blk_exp = jnp.minimum(jnp.searchsorted(pend, jnp.arange(n_blocks) * MOE_BLOCK, side='right'),
                          N_EXPERTS - 1).astype(jnp.int32)
    n_used = (pend[-1] // MOE_BLOCK).astype(jnp.int32).reshape(1)
    return blk_exp, row_word, n_used, row_gate.reshape(rows, 1)


def _combine_kernel(x_ref, y_ref, mod_ref, g_ref, o_ref, *, final):
    y = (y_ref[0] + y_ref[1]) + (y_ref[2] + y_ref[3])
    x = x_ref[...] + mod_ref[5:6, :] * y
    if final:
        x = _rmsnorm_rows(x, g_ref[...])
    o_ref[...] = x


def _combine(x, y4, mod, g, layer, *, batch, seq, rows, final, tm=256):
    d = x.shape[1]
    nl = batch * seq
    row = functools.partial(_mod_row, tm=tm, nl=nl, s=seq, b=batch)
    return pl.pallas_call(
        functools.partial(_combine_kernel, final=final),
        grid=(rows // tm,),
        in_specs=[pl.BlockSpec((tm, d), lambda i: (i, 0)),
                  pl.BlockSpec((TOP_K, tm, d), lambda i: (0, i, 0)),
                  pl.BlockSpec((None, None, 6, d), lambda i: (layer, row(i), 0, 0)),
                  pl.BlockSpec((1, d), lambda i: (0, 0))],
        out_specs=pl.BlockSpec((tm, d), lambda i: (i, 0)),
        out_shape=SDS((rows, d), F32),
        compiler_params=_cparams(("parallel",)),
        name="moe_combine",
    )(x, y4, mod, g)


def _mix_inputs(p, ps, l, attn_sink, cos, sin, dn_conv_w, dn_a_log, dn_dt_bias, lru_conv_w,
                lru_conv_b, lru_w_rgate, lru_b_rgate, lru_w_igate, lru_b_igate, lru_lambda,
                last, geo):
    att = _attention(p, attn_sink[l], cos, sin, with_ctx=not last, **geo)
    u, w, qe, kdt, qk, eg = _dn_prep(p, ps, dn_conv_w[l], dn_a_log[l], dn_dt_bias[l], **geo)
    dn_f, dn_b = _dn_scan(u, w, qe, kdt, qk, eg, **geo)
    lru = [_lru(p, lru_conv_w[l], lru_conv_b[l].reshape(1, -1),
                _block_diag(lru_w_rgate[l, d]).astype(BF16), lru_b_rgate[l, d].reshape(1, -1),
                _block_diag(lru_w_igate[l, d]).astype(BF16), lru_b_igate[l, d].reshape(1, -1),
                lru_lambda[l, d].reshape(1, -1), reverse=bool(d), **geo) for d in range(2)]
    return att, dn_f, dn_b, lru[0], lru[1]


def kernel(x, c, ctx, c_ctx, w_mod, b_mod, norm_mix_g, norm_ffn_g, w_in, w_out, attn_sink,
           dn_conv_w, dn_a_log, dn_dt_bias, dn_norm_g, lru_conv_w, lru_conv_b,
           lru_w_rgate, lru_b_rgate, lru_w_igate, lru_b_igate, lru_lambda,
           w_router, b_router, w_gu, b_gu, w_down, b_down, final_norm_g):
    batch, seq, d = x.shape
    ctx_len = ctx.shape[1]
    depth = w_mod.shape[0]
    nl = batch * seq
    geo = dict(batch=batch, seq=seq, ctx_len=ctx_len)

    c_all = jnp.zeros((8, d), F32).at[:batch].set(c).at[batch].set(c_ctx)
    mod = _modulation(c_all, w_mod, b_mod).reshape(depth, 8, 6, d)
    cos, sin = _rope_tables(seq)
    xa = jnp.concatenate([x.reshape(nl, d), ctx.reshape(batch * ctx_len, d)], axis=0)

    split = COL_LRU_X
    nsmall = 4 * DN_HEADS
    for l in range(depth):
        last = l == depth - 1
        w_main = jnp.concatenate([w_in[l][:, :split], w_in[l][:, split + nsmall:]], axis=1).astype(BF16)
        w_small = jnp.pad(w_in[l][:, split:split + nsmall], ((0, 0), (0, SMALL_W - nsmall))).astype(BF16)
        p, ps = _in_proj(xa, mod, norm_mix_g[l].reshape(1, d), w_main, w_small, l,
                         batch=batch, seq=seq)
        att, dn_f, dn_b, lru_f, lru_b = _mix_inputs(
            p, ps, l, attn_sink, cos, sin, dn_conv_w, dn_a_log, dn_dt_bias, lru_conv_w, lru_conv_b,
            lru_w_rgate, lru_b_rgate, lru_w_igate, lru_b_igate, lru_lambda, last, geo)
        rows = nl if last else xa.shape[0]
        w_r = jnp.pad(w_router[l], ((0, 0), (0, LANES - N_EXPERTS)))
        b_r = jnp.pad(b_router[l], (0, LANES - N_EXPERTS)).reshape(1, LANES)
        x_mid, h_ffn, logits = _out_proj(att, dn_f, dn_b, p, lru_f, lru_b, xa, w_out[l].astype(BF16),
                                         mod, norm_ffn_g[l].reshape(1, d), dn_norm_g[l].reshape(1, -1),
                                         w_r, b_r, l, batch=batch, seq=seq, rows=rows)
        blk_exp, row_word, n_used, row_gate = _route(logits[:, :N_EXPERTS])
        y4 = _moe(h_ffn, blk_exp, row_word, n_used, row_gate, w_gu[l].astype(BF16), b_gu[l],
                  w_down[l].astype(BF16), b_down[l])
        xa = _combine(x_mid, y4, mod, final_norm_g.reshape(1, d), l, batch=batch, seq=seq,
                      rows=rows, final=last)
    return xa.reshape(batch, seq, d)
```

```python
import functools
import math

import jax
import jax.numpy as jnp
from jax import lax
from jax.experimental import pallas as pl
from jax.experimental.pallas import tpu as pltpu

F32 = jnp.float32
BF16 = jnp.bfloat16
HIGHEST = lax.Precision.HIGHEST
SDS = jax.ShapeDtypeStruct

GRID_W = 64
ATT_HEADS = 16
ATT_KV_HEADS = 4
ATT_HEAD_DIM = 64
ATT_BLOCK = 128
ROPE_BASE = 10000.0
ATT_Q_W = ATT_HEADS * ATT_HEAD_DIM
ATT_KV_W = ATT_KV_HEADS * ATT_HEAD_DIM
DN_HEADS = 4
DN_HEAD_DIM = 128
DN_W = DN_HEADS * DN_HEAD_DIM
DN_CHUNK = 128
LRU_WIDTH = 512
LRU_C = 8.0
CONV_W = 4
CONV_LEFT = CONV_W // 2
N_EXPERTS = 32
TOP_K = 4
D_EXPERT = 1024
SWIGLU_LIMIT = 7.0
SWIGLU_ALPHA = 1.702
MOE_BLOCK = 256
EPS = 1e-6
NEG_INF = -1e30

COL_ATT_Q = 0
COL_ATT_K = ATT_Q_W
COL_ATT_V = COL_ATT_K + ATT_KV_W
COL_DN_QKV = COL_ATT_V + ATT_KV_W
COL_DN_GATE = COL_DN_QKV + 3 * DN_W
COL_LRU_X = COL_DN_GATE + DN_W
COL_LRU_Y = COL_LRU_X + LRU_WIDTH
MAIN_W = COL_LRU_Y + LRU_WIDTH
SMALL_W = 128
LANES = 128
HALO = 8
VMEM_LIMIT = 56 * 1024 * 1024


def _cparams(sem):
    return pltpu.CompilerParams(dimension_semantics=sem, vmem_limit_bytes=VMEM_LIMIT)


def _dot(a, b, precision=None):
    return jnp.dot(a, b, preferred_element_type=F32, precision=precision)


def _dot_nt(a, b, precision=None):
    return lax.dot_general(a, b, (((1,), (1,)), ((), ())), preferred_element_type=F32,
                           precision=precision)


def _softplus(x):
    return jnp.maximum(x, 0.0) + jnp.log1p(jnp.exp(-jnp.abs(x)))


def _rmsnorm_rows(x, g):
    return x * lax.rsqrt(jnp.mean(x * x, axis=-1, keepdims=True) + EPS) * g


def _mod_kernel(c_ref, w_ref, b_ref, o_ref):
    cs = c_ref[...]
    s = cs * jax.nn.sigmoid(cs)
    o_ref[0] = _dot(s, w_ref[0], HIGHEST) + b_ref[0]


def _modulation(c_all, w_mod, b_mod, tn=1024):
    depth, d, n = w_mod.shape
    return pl.pallas_call(
        _mod_kernel,
        grid=(depth, n // tn),
        in_specs=[pl.BlockSpec((8, d), lambda l, j: (0, 0)),
                  pl.BlockSpec((1, d, tn), lambda l, j: (l, 0, j)),
                  pl.BlockSpec((1, 1, tn), lambda l, j: (l, 0, j))],
        out_specs=pl.BlockSpec((1, 8, tn), lambda l, j: (l, 0, j)),
        out_shape=SDS((depth, 8, n), F32),
        compiler_params=_cparams(("parallel", "parallel")),
        name="modulation",
    )(c_all, w_mod, b_mod.reshape(depth, 1, n))


def _mod_row(i, tm, nl, s, b):
    return jnp.where(i * tm < nl, (i * tm) // s, b)


def _in_proj_kernel(x_ref, mod_ref, g_ref, w_ref, ws_ref, o_ref, os_ref, xn_ref):
    @pl.when(pl.program_id(1) == 0)
    def _():
        hn = _rmsnorm_rows(x_ref[...], g_ref[...]) * (1.0 + mod_ref[1:2, :]) + mod_ref[0:1, :]
        hb = hn.astype(BF16)
        xn_ref[...] = hb
        os_ref[...] = _dot(hb, ws_ref[...])

    o_ref[...] = _dot(xn_ref[...], w_ref[...])


def _in_proj(x, mod, g, w_main, w_small, layer, *, batch, seq, tn=768):
    m, d = x.shape
    n = w_main.shape[1]
    nl = batch * seq
    tm = math.gcd(math.gcd(nl, m - nl), 1024)
    row = functools.partial(_mod_row, tm=tm, nl=nl, s=seq, b=batch)
    return pl.pallas_call(
        _in_proj_kernel,
        grid=(m // tm, n // tn),
        in_specs=[pl.BlockSpec((tm, d), lambda i, j: (i, 0)),
                  pl.BlockSpec((None, None, 6, d), lambda i, j: (layer, row(i), 0, 0)),
                  pl.BlockSpec((1, d), lambda i, j: (0, 0)),
                  pl.BlockSpec((d, tn), lambda i, j: (0, j)),
                  pl.BlockSpec((d, SMALL_W), lambda i, j: (0, 0))],
        out_specs=[pl.BlockSpec((tm, tn), lambda i, j: (i, j)),
                   pl.BlockSpec((tm, SMALL_W), lambda i, j: (i, 0))],
        out_shape=[SDS((m, n), F32), SDS((m, SMALL_W), F32)],
        scratch_shapes=[pltpu.VMEM((tm, d), BF16)],
        compiler_params=_cparams(("parallel", "arbitrary")),
        name="in_proj",
    )(x, mod, g, w_main, w_small)


def _rope_tables(seq):
    t = jnp.arange(seq)
    half = ATT_HEAD_DIM // 2
    inv = ROPE_BASE ** (-jnp.arange(0, half, 2, dtype=F32) / half)
    ang_r = (t // GRID_W).astype(F32)[:, None] * inv
    ang_c = (t % GRID_W).astype(F32)[:, None] * inv
    cr, sr, cc, sc = jnp.cos(ang_r), jnp.sin(ang_r), jnp.cos(ang_c), jnp.sin(ang_c)
    cos = jnp.concatenate([cr, cr, cc, cc], axis=-1)
    sin = jnp.concatenate([-sr, sr, -sc, sc], axis=-1)
    cos = jnp.concatenate([jnp.tile(cos, (1, 2)), jnp.ones((ATT_BLOCK, LANES), F32)], axis=0)
    sin = jnp.concatenate([jnp.tile(sin, (1, 2)), jnp.zeros((ATT_BLOCK, LANES), F32)], axis=0)
    return cos, sin


def _rope(x, cos, sin):
    w = x.shape[1]
    rep = w // LANES
    c = jnp.tile(cos, (1, rep)) if rep > 1 else cos
    s = jnp.tile(sin, (1, rep)) if rep > 1 else sin
    lane = lax.broadcasted_iota(jnp.int32, x.shape, 1)
    quarter = ATT_HEAD_DIM // 4
    first = (lane % (2 * quarter)) < quarter
    swapped = jnp.where(first, pltpu.roll(x, w - quarter, 1), pltpu.roll(x, quarter, 1))
    return x * c + swapped * s


def _attn_kernel(sink_ref, q_ref, kp_ref, kc_ref, kn_ref, kx_ref, vp_ref, vc_ref, vn_ref, vx_ref,
                 cq_ref, sq_ref, cp_ref, sp_ref, cc_ref, sc_ref, cn_ref, sn_ref, o_ref,
                 *, n_lat_blk, blk_per_seq):
    qb = pl.program_id(0)
    is_lat = qb < n_lat_blk
    i = qb % blk_per_seq
    blk = ATT_BLOCK
    scale = ATT_HEAD_DIM ** -0.5

    q = _rope(q_ref[...], cq_ref[...], sq_ref[...]) * scale
    kp = _rope(kp_ref[...], cp_ref[...], sp_ref[...])
    kc = _rope(kc_ref[...], cc_ref[...], sc_ref[...])
    kn = _rope(kn_ref[...], cn_ref[...], sn_ref[...])
    kall = jnp.concatenate([kp, kc, kn, kx_ref[...]], axis=0)
    vall = jnp.concatenate([vp_ref[...], vc_ref[...], vn_ref[...], vx_ref[...]], axis=0)
    nkeys = kall.shape[0]

    off = 2 * blk
    r = lax.broadcasted_iota(jnp.int32, (blk, blk), 0)
    n = lax.broadcasted_iota(jnp.int32, (blk, blk), 1)
    m_prev = n >= r + jnp.where(is_lat & (i >= 1), 0, off)
    m_cur = n >= jnp.where(is_lat, 0, off)
    m_next = n <= r - jnp.where(is_lat & (i < blk_per_seq - 1), 0, off)
    m_ctx = jnp.full((blk, nkeys - 3 * blk), True)
    mask = jnp.concatenate([m_prev, m_cur, m_next, m_ctx], axis=1)

    lane = lax.broadcasted_iota(jnp.int32, (nkeys, LANES), 1)
    low = lane < ATT_HEAD_DIM
    for hk in range(ATT_KV_HEADS):
        grp = (hk // 2) * LANES
        kg = kall[:, grp:grp + LANES]
        vg = vall[:, grp:grp + LANES]
        kr = pltpu.roll(kg, ATT_HEAD_DIM, 1)
        vr = pltpu.roll(vg, ATT_HEAD_DIM, 1)
        if hk % 2 == 0:
            k_lo, k_hi, v_lo, v_hi = kg, kr, vg, vr
        else:
            k_lo, k_hi, v_lo, v_hi = kr, kg, vr, vg
        halves = ((jnp.where(low, k_lo, 0.0).astype(BF16), jnp.where(low, v_lo, 0.0).astype(BF16)),
                  (jnp.where(low, 0.0, k_hi).astype(BF16), jnp.where(low, 0.0, v_hi).astype(BF16)))
        for p in range(2):
            col = hk * 2 * LANES + p * LANES
            qp = q[:, col:col + LANES].astype(BF16)
            acc = jnp.zeros((blk, LANES), F32)
            for half in range(2):
                h = hk * 4 + p * 2 + half
                kh, vh = halves[half]
                sink = sink_ref[h]
                lg = jnp.where(mask, _dot_nt(qp, kh), NEG_INF)
                mx = jnp.maximum(jnp.max(lg, axis=-1, keepdims=True), sink)
                pe = jnp.exp(lg - mx)
                den = jnp.sum(pe, axis=-1, keepdims=True) + jnp.exp(sink - mx)
                acc = acc + _dot(pe.astype(BF16), vh) / den
            o_ref[:, col:col + LANES] = acc.astype(o_ref.dtype)


def _attention(p, sink, cos, sin, *, batch, seq, ctx_len, with_ctx):
    blk = ATT_BLOCK
    nl = batch * seq
    n_lat_blk = nl // blk
    bps = seq // blk
    cps = ctx_len // blk
    nq = n_lat_blk + (batch * cps if with_ctx else 0)
    kcol, vcol = COL_ATT_K // ATT_KV_W, COL_ATT_V // ATT_KV_W

    def parts(qb):
        is_lat = qb < n_lat_blk
        b = jnp.where(is_lat, qb // bps, (qb - n_lat_blk) // cps)
        i = jnp.where(is_lat, qb % bps, 0)
        return is_lat, b, i

    def kblk(qb, off):
        is_lat, b, i = parts(qb)
        return jnp.where(is_lat, b * bps + jnp.clip(i + off, 0, bps - 1), 0)

    def posblk(qb, off):
        is_lat, _, i = parts(qb)
        return jnp.where(is_lat, jnp.clip(i + off, 0, bps - 1), bps)

    def ctxblk(qb):
        return nl // ctx_len + parts(qb)[1]

    kv = lambda off, col: pl.BlockSpec((blk, ATT_KV_W), lambda qb: (kblk(qb, off), col))
    cx = lambda col: pl.BlockSpec((ctx_len, ATT_KV_W), lambda qb: (ctxblk(qb), col))
    tab = lambda off: pl.BlockSpec((blk, LANES), lambda qb: (posblk(qb, off), 0))
    in_specs = [pl.BlockSpec(memory_space=pltpu.SMEM),
                pl.BlockSpec((blk, ATT_Q_W), lambda qb: (qb, 0)),
                kv(-1, kcol), kv(0, kcol), kv(1, kcol), cx(kcol),
                kv(-1, vcol), kv(0, vcol), kv(1, vcol), cx(vcol),
                tab(0), tab(0), tab(-1), tab(-1), tab(0), tab(0), tab(1), tab(1)]
    return pl.pallas_call(
        functools.partial(_attn_kernel, n_lat_blk=n_lat_blk, blk_per_seq=bps),
        grid=(nq,),
        in_specs=in_specs,
        out_specs=pl.BlockSpec((blk, ATT_Q_W), lambda qb: (qb, 0)),
        out_shape=SDS((nq * blk, ATT_Q_W), BF16),
        compiler_params=_cparams(("parallel",)),
        name="attention",
    )(sink, p, p, p, p, p, p, p, p, p, cos, sin, cos, sin, cos, sin, cos, sin)


def _conv_tile(x_ref, prev_ref, next_ref, w_ref, buf_ref, first, last):
    tm = x_ref.shape[0]
    buf_ref[0:HALO, :] = prev_ref[...] * jnp.where(first, 0.0, 1.0)
    buf_ref[HALO:HALO + tm, :] = x_ref[...]
    buf_ref[HALO + tm:2 * HALO + tm, :] = next_ref[...] * jnp.where(last, 0.0, 1.0)
    out = None
    for j in range(CONV_W):
        start = HALO - CONV_LEFT + j
        term = buf_ref[start:start + tm, :] * w_ref[j:j + 1, :]
        out = term if out is None else out + term
    return out


def _lru_kernel(x_ref, prev_ref, next_ref, cw_ref, cb_ref, wr_ref, br_ref, wi_ref, bi_ref,
                lam_ref, o_ref, buf_ref, carry_ref, *, ctx_tiles, lat_tiles, reverse):
    t = pl.program_id(1)
    tm = x_ref.shape[0]
    in_ctx = t < ctx_tiles
    j = jnp.where(in_ctx, t, t - ctx_tiles)
    nseg = jnp.where(in_ctx, ctx_tiles, lat_tiles)
    jj = nseg - 1 - j if reverse else j
    xc = _conv_tile(x_ref, prev_ref, next_ref, cw_ref, buf_ref, jj == 0, jj == nseg - 1)
    xc = xc + cb_ref[...]
    xb = xc.astype(BF16)
    r = jax.nn.sigmoid(_dot(xb, wr_ref[...]) + br_ref[...])
    gi = jax.nn.sigmoid(_dot(xb, wi_ref[...]) + bi_ref[...])
    log_sig = -_softplus(-lam_ref[...])
    log_a = LRU_C * r * log_sig
    a = jnp.exp(log_a)
    u = jnp.sqrt(1.0 - jnp.exp(2.0 * log_a)) * (gi * xc)

    row = lax.broadcasted_iota(jnp.int32, (tm, 1), 0)
    s = 1
    while s < tm:
        shift = tm - s if reverse else s
        a_sh, u_sh = pltpu.roll(a, shift, 0), pltpu.roll(u, shift, 0)
        valid = (row < tm - s) if reverse else (row >= s)
        u = jnp.where(valid, a * u_sh + u, u)
        a = jnp.where(valid, a * a_sh, a)
        s *= 2

    @pl.when(t == 0)
    def _():
        carry_ref[...] = jnp.zeros_like(carry_ref)

    h = a * carry_ref[0:1, :] + u
    o_ref[...] = h
    carry_ref[0:1, :] = h[0:1, :] if reverse else h[tm - 1:tm, :]


def _lru(p, conv_w, conv_b, w_r, b_r, w_i, b_i, lam, *, batch, seq, ctx_len, reverse, tm=256):
    m = p.shape[0]
    nl = batch * seq
    lat_tiles, ctx_tiles = seq // tm, ctx_len // tm
    steps = lat_tiles + ctx_tiles
    xcol = COL_LRU_X // LRU_WIDTH
    hpt = tm // HALO
    nh = m // HALO

    def tile(b, t):
        in_ctx = t < ctx_tiles
        j = jnp.where(in_ctx, t, t - ctx_tiles)
        nseg = jnp.where(in_ctx, ctx_tiles, lat_tiles)
        jj = nseg - 1 - j if reverse else j
        return jnp.where(in_ctx, nl // tm + b * ctx_tiles + jj, b * lat_tiles + jj)

    const = lambda shape: pl.BlockSpec(shape, lambda b, t: (0,) * len(shape))
    return pl.pallas_call(
        functools.partial(_lru_kernel, ctx_tiles=ctx_tiles, lat_tiles=lat_tiles, reverse=reverse),
        grid=(batch, steps),
        in_specs=[pl.BlockSpec((tm, LRU_WIDTH), lambda b, t: (tile(b, t), xcol)),
                  pl.BlockSpec((HALO, LRU_WIDTH),
                               lambda b, t: (jnp.maximum(tile(b, t) * hpt - 1, 0), xcol)),
                  pl.BlockSpec((HALO, LRU_WIDTH),
                               lambda b, t: (jnp.minimum((tile(b, t) + 1) * hpt, nh - 1), xcol)),
                  const((CONV_W, LRU_WIDTH)), const((1, LRU_WIDTH)),
                  const((LRU_WIDTH, LRU_WIDTH)), const((1, LRU_WIDTH)),
                  const((LRU_WIDTH, LRU_WIDTH)), const((1, LRU_WIDTH)),
                  const((1, LRU_WIDTH))],
        out_specs=pl.BlockSpec((tm, LRU_WIDTH), lambda b, t: (tile(b, t), 0)),
        out_shape=SDS((m, LRU_WIDTH), F32),
        scratch_shapes=[pltpu.VMEM((tm + 2 * HALO, LRU_WIDTH), F32),
                        pltpu.VMEM((8, LRU_WIDTH), F32)],
        compiler_params=_cparams(("parallel", "arbitrary")),
        name="rglru_bwd" if reverse else "rglru_fwd",
    )(p, p, p, conv_w, conv_b, w_r, b_r, w_i, b_i, lam)


def _block_diag(w):
    nb, k, _ = w.shape
    eye = jnp.eye(nb, dtype=w.dtype)
    return jnp.einsum('nij,nm->nimj', w, eye).reshape(nb * k, nb * k)


def _dn_prep_kernel(x_ref, prev_ref, next_ref, ps_ref, cw_ref, alog_ref, dt_ref,
                    u_ref, w_ref, qe_ref, kdt_ref, qk_ref, eg_ref, buf_ref, qkv_ref,
                    *, nl, lat_tiles, ctx_tiles):
    tm = x_ref.shape[0]
    c_sz = DN_CHUNK
    i = pl.program_id(0)
    is_lat = i * tm < nl
    j = jnp.where(is_lat, i % lat_tiles, (i - nl // tm) % ctx_tiles)
    nseg = jnp.where(is_lat, lat_tiles, ctx_tiles)
    act = _conv_tile(x_ref, prev_ref, next_ref, cw_ref, buf_ref, j == 0, j == nseg - 1)
    act = act * jax.nn.sigmoid(act)
    for h in range(DN_HEADS):
        lo = h * DN_HEAD_DIM
        qh = act[:, lo:lo + DN_HEAD_DIM]
        kh = act[:, DN_W + lo:DN_W + lo + DN_HEAD_DIM]
        qkv_ref[:, lo:lo + DN_HEAD_DIM] = (
            qh * lax.rsqrt(jnp.sum(qh * qh, axis=-1, keepdims=True) + EPS) * DN_HEAD_DIM ** -0.5)
        qkv_ref[:, DN_W + lo:DN_W + lo + DN_HEAD_DIM] = (
            kh * lax.rsqrt(jnp.sum(kh * kh, axis=-1, keepdims=True) + EPS))
    qkv_ref[:, 2 * DN_W:3 * DN_W] = act[:, 2 * DN_W:3 * DN_W]

    nhd = 2 * DN_HEADS
    ps = ps_ref[...]
    beta_c = jax.nn.sigmoid(ps[:, 0:nhd])
    ri = lax.broadcasted_iota(jnp.int32, (c_sz, c_sz), 0)
    ci = lax.broadcasted_iota(jnp.int32, (c_sz, c_sz), 1)
    lower_incl = ci <= ri
    upper_incl = ci >= ri
    g_full = jnp.where((ci >= nhd) & (ci < 2 * nhd),
                       -jnp.exp(alog_ref[...]) * _softplus(ps + dt_ref[...]), 0.0)
    lane_fwd = ci < nhd + DN_HEADS
    gcum_c = jnp.where(lane_fwd, _dot(lower_incl.astype(F32), g_full, HIGHEST),
                       _dot(upper_incl.astype(F32), g_full, HIGHEST))
    gcum_r = gcum_c.T

    xor = ri ^ ci
    levels = []
    size = 1
    while size < c_sz:
        levels.append((xor >= size) & (xor < 2 * size))
        size *= 2

    bodies = []
    for h in range(DN_HEADS):
        lo = h * DN_HEAD_DIM
        qh = qkv_ref[:, lo:lo + DN_HEAD_DIM]
        kh = qkv_ref[:, DN_W + lo:DN_W + lo + DN_HEAD_DIM]
        vh = qkv_ref[:, 2 * DN_W + lo:2 * DN_W + lo + DN_HEAD_DIM]
        khb = kh.astype(BF16)
        kk = _dot_nt(khb, khb)
        qk_raw = _dot_nt(qh.astype(BF16), khb)
        for d in range(2):
            idx = d * DN_HEADS + h
            incl = lower_incl if d == 0 else upper_incl
            strict = (ci < ri) if d == 0 else (ci > ri)
            end_row = c_sz - 1 if d == 0 else 0
            beta = beta_c[:, idx:idx + 1]
            gc = gcum_c[:, nhd + idx:nhd + idx + 1]
            gr = gcum_r[nhd + idx:nhd + idx + 1, :]
            gtot = gc[end_row:end_row + 1, :]
            dec = jnp.exp(jnp.where(incl, gc - gr, -jnp.inf))
            a_mat = jnp.where(strict, beta * kk * dec, 0.0)
            eg = jnp.exp(gc)
            rhs = jnp.concatenate([vh * beta, kh * (beta * eg)], axis=1)
            qe_ref[d, :, lo:lo + DN_HEAD_DIM] = (qh * eg).astype(BF16)
            kdt_ref[d, 0, h] = (kh * jnp.exp(gtot - gc)).T.astype(BF16)
            qk_ref[d, 0, h] = (qk_raw * dec).astype(BF16)
            eg_ref[0, idx:idx + 1, :] = jnp.broadcast_to(jnp.exp(gtot), (1, LANES))
            bodies.append((d, lo, a_mat, rhs))

    errs = [-jnp.where(levels[0], a_mat, 0.0) for _, _, a_mat, _ in bodies]
    for lvl in levels[1:]:
        aos = [jnp.where(lvl, a_mat, 0.0) for _, _, a_mat, _ in bodies]
        ebs = [e.astype(BF16) for e in errs]
        gms = [ao + _dot(eb, ao.astype(BF16)) for ao, eb in zip(aos, ebs)]
        errs = [e - (gm + _dot(gm.astype(BF16), eb)) for e, gm, eb in zip(errs, gms, ebs)]

    for (d, lo, _, rhs), e in zip(bodies, errs):
        sol = rhs + _dot(e.astype(BF16), rhs.astype(BF16))
        u_ref[d, :, lo:lo + DN_HEAD_DIM] = sol[:, :DN_HEAD_DIM]
        w_ref[d, :, lo:lo + DN_HEAD_DIM] = sol[:, DN_HEAD_DIM:].astype(BF16)


def _dn_prep(p, ps, conv_w, a_log, dt_bias, *, batch, seq, ctx_len):
    m = p.shape[0]
    nl = batch * seq
    tm = DN_CHUNK
    cpt = 1
    nchunks = m // DN_CHUNK
    hpt = tm // HALO
    nh = m // HALO
    qcol = COL_DN_QKV // (3 * DN_W)
    nhd = 2 * DN_HEADS
    lane_pad = lambda a: jnp.zeros((1, SMALL_W), F32).at[0, nhd:2 * nhd].set(a.reshape(nhd))
    const = lambda shape: pl.BlockSpec(shape, lambda i: (0,) * len(shape))
    return pl.pallas_call(
        functools.partial(_dn_prep_kernel, nl=nl, lat_tiles=seq // tm, ctx_tiles=ctx_len // tm),
        grid=(m // tm,),
        in_specs=[pl.BlockSpec((tm, 3 * DN_W), lambda i: (i, qcol)),
                  pl.BlockSpec((HALO, 3 * DN_W), lambda i: (jnp.maximum(i * hpt - 1, 0), qcol)),
                  pl.BlockSpec((HALO, 3 * DN_W), lambda i: (jnp.minimum((i + 1) * hpt, nh - 1), qcol)),
                  pl.BlockSpec((tm, SMALL_W), lambda i: (i, 0)),
                  const((CONV_W, 3 * DN_W)), const((1, SMALL_W)), const((1, SMALL_W))],
        out_specs=[pl.BlockSpec((2, tm, DN_W), lambda i: (0, i, 0)),
                   pl.BlockSpec((2, tm, DN_W), lambda i: (0, i, 0)),
                   pl.BlockSpec((2, tm, DN_W), lambda i: (0, i, 0)),
                   pl.BlockSpec((2, cpt, DN_HEADS, DN_HEAD_DIM, DN_CHUNK), lambda i: (0, i, 0, 0, 0)),
                   pl.BlockSpec((2, cpt, DN_HEADS, DN_CHUNK, DN_CHUNK), lambda i: (0, i, 0, 0, 0)),
                   pl.BlockSpec((cpt, 8, LANES), lambda i: (i, 0, 0))],
        out_shape=[SDS((2, m, DN_W), F32), SDS((2, m, DN_W), BF16), SDS((2, m, DN_W), BF16),
                   SDS((2, nchunks, DN_HEADS, DN_HEAD_DIM, DN_CHUNK), BF16),
                   SDS((2, nchunks, DN_HEADS, DN_CHUNK, DN_CHUNK), BF16),
                   SDS((nchunks, 8, LANES), F32)],
        scratch_shapes=[pltpu.VMEM((tm + 2 * HALO, 3 * DN_W), F32),
                        pltpu.VMEM((tm, 3 * DN_W), F32)],
        compiler_params=_cparams(("parallel",)),
        name="deltanet_prep",
    )(p, p, p, ps, conv_w, lane_pad(a_log), lane_pad(dt_bias))


def _dn_scan_kernel(uf_ref, wf_ref, qef_ref, kdtf_ref, qkf_ref, egf_ref,
                    ub_ref, wb_ref, qeb_ref, kdtb_ref, qkb_ref, egb_ref, of_ref, ob_ref, s_ref):
    @pl.when(pl.program_id(1) == 0)
    def _():
        s_ref[...] = jnp.zeros_like(s_ref)

    dirs = ((uf_ref, wf_ref, qef_ref, kdtf_ref, qkf_ref, egf_ref, of_ref),
            (ub_ref, wb_ref, qeb_ref, kdtb_ref, qkb_ref, egb_ref, ob_ref))
    for d, (u_ref, w_ref, qe_ref, kdt_ref, qk_ref, eg_ref, o_ref) in enumerate(dirs):
        for h in range(DN_HEADS):
            idx = d * DN_HEADS + h
            lo = h * DN_HEAD_DIM
            st = s_ref[idx]
            sb = st.astype(BF16)
            v_new = u_ref[:, lo:lo + DN_HEAD_DIM] - _dot(w_ref[:, lo:lo + DN_HEAD_DIM], sb)
            vb = v_new.astype(BF16)
            o_ref[:, lo:lo + DN_HEAD_DIM] = (
                _dot(qe_ref[:, lo:lo + DN_HEAD_DIM], sb) + _dot(qk_ref[h], vb))
            s_ref[idx] = st * eg_ref[idx:idx + 1, :] + _dot(kdt_ref[h], vb)


def _dn_scan(u, w, qe, kdt, qk, eg, *, batch, seq, ctx_len):
    m = u.shape[1]
    c_sz = DN_CHUNK
    nl = batch * seq
    lat_ch, ctx_ch = seq // c_sz, ctx_len // c_sz
    steps = lat_ch + ctx_ch

    def chunk(d, b, s):
        in_ctx = s < ctx_ch
        j = jnp.where(in_ctx, s, s - ctx_ch)
        nseg = jnp.where(in_ctx, ctx_ch, lat_ch)
        jj = j if d == 0 else nseg - 1 - j
        return jnp.where(in_ctx, nl // c_sz + b * ctx_ch + jj, b * lat_ch + jj)

    def specs(d):
        rows = lambda width: pl.BlockSpec((None, c_sz, width), lambda b, s: (d, chunk(d, b, s), 0))
        return [rows(DN_W), rows(DN_W), rows(DN_W),
                pl.BlockSpec((None, None, DN_HEADS, DN_HEAD_DIM, c_sz),
                             lambda b, s: (d, chunk(d, b, s), 0, 0, 0)),
                pl.BlockSpec((None, None, DN_HEADS, c_sz, c_sz),
                             lambda b, s: (d, chunk(d, b, s), 0, 0, 0)),
                pl.BlockSpec((None, 8, LANES), lambda b, s: (chunk(d, b, s), 0, 0))]

    out_spec = lambda d: pl.BlockSpec((c_sz, DN_W), lambda b, s: (chunk(d, b, s), 0))
    return pl.pallas_call(
        _dn_scan_kernel,
        grid=(batch, steps),
        in_specs=specs(0) + specs(1),
        out_specs=[out_spec(0), out_spec(1)],
        out_shape=[SDS((m, DN_W), F32), SDS((m, DN_W), F32)],
        scratch_shapes=[pltpu.VMEM((2 * DN_HEADS, DN_HEAD_DIM, DN_HEAD_DIM), F32)],
        compiler_params=_cparams(("parallel", "arbitrary")),
        name="deltanet_scan",
    )(u, w, qe, kdt, qk, eg, u, w, qe, kdt, qk, eg)


def _out_proj_kernel(att_ref, dnf_ref, dnb_ref, dng_ref, lf_ref, lb_ref, ly_ref, x_ref, w_ref,
                     mod_ref, gffn_ref, dn_g_ref, wr_ref, br_ref, xo_ref, h_ref, lg_ref):
    o = dnf_ref[...] + dnb_ref[...]
    gate = dng_ref[...]
    dn_parts = []
    for h in range(DN_HEADS):
        lo = h * DN_HEAD_DIM
        oh = o[:, lo:lo + DN_HEAD_DIM]
        gh = gate[:, lo:lo + DN_HEAD_DIM]
        y = oh * lax.rsqrt(jnp.mean(oh * oh, axis=-1, keepdims=True) + EPS) * dn_g_ref[...]
        dn_parts.append((y * (gh * jax.nn.sigmoid(gh))).astype(BF16))
    dn = jnp.concatenate(dn_parts, axis=1)
    lru = ((lf_ref[...] + lb_ref[...]) * jax.nn.gelu(ly_ref[...])).astype(BF16)
    acc = (_dot(att_ref[...], w_ref[0:ATT_Q_W, :])
           + _dot(dn, w_ref[ATT_Q_W:ATT_Q_W + DN_W, :])
           + _dot(lru, w_ref[ATT_Q_W + DN_W:, :]))
    xn = x_ref[...] + mod_ref[2:3, :] * acc
    xo_ref[...] = xn
    hn = _rmsnorm_rows(xn, gffn_ref[...]) * (1.0 + mod_ref[4:5, :]) + mod_ref[3:4, :]
    h_ref[...] = hn
    lg_ref[...] = _dot(hn, wr_ref[...], HIGHEST) + br_ref[...]


def _out_proj(att, dn_f, dn_b, p, lru_f, lru_b, x, w_out, mod, g_ffn, dn_g, w_router, b_router,
              layer, *, batch, seq, rows, tm=256):
    d = x.shape[1]
    nl = batch * seq
    row = functools.partial(_mod_row, tm=tm, nl=nl, s=seq, b=batch)
    gcol, ycol = COL_DN_GATE // DN_W, COL_LRU_Y // LRU_WIDTH
    const = lambda shape: pl.BlockSpec(shape, lambda i: (0,) * len(shape))
    tile = lambda width, col=0: pl.BlockSpec((tm, width), lambda i: (i, col))
    return pl.pallas_call(
        _out_proj_kernel,
        grid=(rows // tm,),
        in_specs=[tile(ATT_Q_W), tile(DN_W), tile(DN_W), tile(DN_W, gcol),
                  tile(LRU_WIDTH), tile(LRU_WIDTH), tile(LRU_WIDTH, ycol), tile(d),
                  const((d, d)),
                  pl.BlockSpec((None, None, 6, d), lambda i: (layer, row(i), 0, 0)),
                  const((1, d)), const((1, DN_HEAD_DIM)), const((d, LANES)), const((1, LANES))],
        out_specs=[tile(d), tile(d), tile(LANES)],
        out_shape=[SDS((rows, d), F32), SDS((rows, d), F32), SDS((rows, LANES), F32)],
        compiler_params=_cparams(("parallel",)),
        name="out_proj",
    )(att, dn_f, dn_b, p, lru_f, lru_b, p, x, w_out, mod, g_ffn, dn_g, w_router, b_router)


ROW_SRC_BITS = 15


def _moe_kernel(exp_ref, row_ref, nused_ref, h_hbm, wgu_ref, bgu_ref, wd_ref, bd_ref, gate_ref,
                o_hbm, x0, x1, y0, y1, gsem, ssem):
    del exp_ref
    i = pl.program_id(0)
    nused = nused_ref[0]
    rows = MOE_BLOCK
    src_mask = (1 << ROW_SRC_BITS) - 1
    xbufs, ybufs = (x0, x1), (y0, y1)

    def gather_row(blk, par, r):
        tok = row_ref[(blk + 1) * rows + r] & src_mask
        pltpu.make_async_copy(h_hbm.at[pl.ds(tok, 1)], xbufs[par].at[pl.ds(r, 1)],
                              gsem.at[par]).start()

    def scatter_row(blk, par, r):
        dst = lax.shift_right_logical(row_ref[(blk + 1) * rows + r], ROW_SRC_BITS)
        pltpu.make_async_copy(ybufs[par].at[pl.ds(r, 1)], o_hbm.at[pl.ds(dst, 1)],
                              ssem.at[par]).start()

    def wait_gather(par):
        pltpu.make_async_copy(h_hbm.at[pl.ds(0, rows)], xbufs[par], gsem.at[par]).wait()

    def wait_scatter(par):
        pltpu.make_async_copy(ybufs[par], o_hbm.at[pl.ds(0, rows)], ssem.at[par]).wait()

    def looped(fn, blk, par):
        def body(r, carry):
            fn(blk, par, r)
            return carry
        lax.fori_loop(0, rows, body, 0)

    @pl.when(i == 0)
    def _():
        looped(gather_row, 0, 0)
        y1[...] = jnp.zeros_like(y1)

    def step(par):
        wait_gather(par)

        @pl.when(i >= 1)
        def _():
            wait_scatter(par)

        for r in range(rows):
            gather_row(i + 1, 1 - par, r)
            scatter_row(i - 1, 1 - par, r)
        xb = xbufs[par][...].astype(BF16)
        gu = _dot(xb, wgu_ref[0]) + bgu_ref[0]
        glu = jnp.minimum(gu[:, :D_EXPERT], SWIGLU_LIMIT)
        lin = jnp.clip(gu[:, D_EXPERT:], -SWIGLU_LIMIT, SWIGLU_LIMIT)
        act = (lin + 1.0) * glu * jax.nn.sigmoid(SWIGLU_ALPHA * glu)
        ybufs[par][...] = (_dot(act.astype(BF16), wd_ref[0]) + bd_ref[0]) * gate_ref[...]

    def drain(par):
        wait_gather(par)
        wait_scatter(par)
        looped(scatter_row, i - 1, 1 - par)
        wait_scatter(1 - par)

    for par in range(2):
        pl.when((i < nused) & (i % 2 == par))(functools.partial(step, par))
        pl.when((i == nused) & (i % 2 == par))(functools.partial(drain, par))


def _moe(h, blk_exp, row_word, n_used, row_gate, w_gu, b_gu, w_down, b_down):
    n_tok, d = h.shape
    steps = blk_exp.shape[0]
    wmap = lambda i, e, r, u: (e[i], 0, 0)
    grid_spec = pltpu.PrefetchScalarGridSpec(
        num_scalar_prefetch=3,
        grid=(steps,),
        in_specs=[pl.BlockSpec(memory_space=pl.ANY),
                  pl.BlockSpec((1, d, 2 * D_EXPERT), wmap),
                  pl.BlockSpec((1, 1, 2 * D_EXPERT), wmap),
                  pl.BlockSpec((1, D_EXPERT, d), wmap),
                  pl.BlockSpec((1, 1, d), wmap),
                  pl.BlockSpec((MOE_BLOCK, 1), lambda i, e, r, u: (i, 0))],
        out_specs=pl.BlockSpec(memory_space=pl.ANY),
        scratch_shapes=[pltpu.VMEM((MOE_BLOCK, d), F32), pltpu.VMEM((MOE_BLOCK, d), F32),
                        pltpu.VMEM((MOE_BLOCK, d), F32), pltpu.VMEM((MOE_BLOCK, d), F32),
                        pltpu.SemaphoreType.DMA((2,)),
                        pltpu.SemaphoreType.DMA((2,))])
    return pl.pallas_call(
        _moe_kernel,
        grid_spec=grid_spec,
        out_shape=SDS((TOP_K * n_tok + MOE_BLOCK, d), F32),
        compiler_params=_cparams(("arbitrary",)),
        name="moe_experts",
    )(blk_exp, row_word, n_used, h, w_gu, b_gu.reshape(N_EXPERTS, 1, -1), w_down,
      b_down.reshape(N_EXPERTS, 1, -1), row_gate)


def _route(logits):
    n = logits.shape[0]
    top_val, top_idx = lax.top_k(logits, TOP_K)
    gates = jax.nn.softmax(top_val, axis=-1)
    e_flat = top_idx.reshape(-1)
    order = jnp.argsort(e_flat)
    counts = jnp.bincount(e_flat, length=N_EXPERTS)
    padded = (counts + MOE_BLOCK - 1) // MOE_BLOCK * MOE_BLOCK
    start = jnp.cumsum(counts) - counts
    pend = jnp.cumsum(padded)
    n_blocks = -(-(n * TOP_K + N_EXPERTS * (MOE_BLOCK - 1)) // MOE_BLOCK)
    blk_exp = jnp.minimum(jnp.searchsorted(pend, jnp.arange(n_blocks) * MOE_BLOCK, side='right'),
                          N_EXPERTS - 1).astype(jnp.int32)
    row = jnp.arange(n_blocks * MOE_BLOCK)
    e_row = jnp.repeat(blk_exp, MOE_BLOCK)
    rank = row - (pend - padded)[e_row]
    valid = rank < counts[e_row]
    assign = order[jnp.clip(start[e_row] + rank, 0, n * TOP_K - 1)]
    spare = TOP_K * n + row % MOE_BLOCK
    src = jnp.where(valid, assign // TOP_K, 0)
    dst = jnp.where(valid, (assign % TOP_K) * n + assign // TOP_K, spare)
    word = (src | (dst << ROW_SRC_BITS)).astype(jnp.int32)
    edge = (spare[:MOE_BLOCK] << ROW_SRC_BITS).astype(jnp.int32)
    row_word = jnp.concatenate([edge, word, edge])
    row_gate = jnp.where(valid, gates.reshape(-1)[assign], 0.0)
    row_gate = jnp.concatenate([row_gate, jnp.zeros((MOE_BLOCK,), F32)]).reshape(-1, 1)
    n_used = (pend[-1] // MOE_BLOCK).astype(jnp.int32).reshape(1)
    return jnp.concatenate([blk_exp, blk_exp[-1:]]), row_word, n_used, row_gate


def _combine_kernel(x_ref, y0_ref, y1_ref, y2_ref, y3_ref, mod_ref, g_ref, o_ref, *, final):
    y = (y0_ref[...] + y1_ref[...]) + (y2_ref[...] + y3_ref[...])
    x = x_ref[...] + mod_ref[5:6, :] * y
    if final:
        x = _rmsnorm_rows(x, g_ref[...])
    o_ref[...] = x


def _combine(x, y4, mod, g, layer, *, batch, seq, rows, final, tm=256):
    d = x.shape[1]
    nl = batch * seq
    row = functools.partial(_mod_row, tm=tm, nl=nl, s=seq, b=batch)
    tiles = rows // tm
    slot = lambda k: pl.BlockSpec((tm, d), lambda i: (k * tiles + i, 0))
    return pl.pallas_call(
        functools.partial(_combine_kernel, final=final),
        grid=(tiles,),
        in_specs=[pl.BlockSpec((tm, d), lambda i: (i, 0)),
                  slot(0), slot(1), slot(2), slot(3),
                  pl.BlockSpec((None, None, 6, d), lambda i: (layer, row(i), 0, 0)),
                  pl.BlockSpec((1, d), lambda i: (0, 0))],
        out_specs=pl.BlockSpec((tm, d), lambda i: (i, 0)),
        out_shape=SDS((rows, d), F32),
        compiler_params=_cparams(("parallel",)),
        name="moe_combine",
    )(x, y4, y4, y4, y4, mod, g)


def _mix_inputs(p, ps, l, attn_sink, cos, sin, dn_conv_w, dn_a_log, dn_dt_bias, lru_conv_w,
                lru_conv_b, lru_w_rgate, lru_b_rgate, lru_w_igate, lru_b_igate, lru_lambda,
                last, geo):
    att = _attention(p, attn_sink[l], cos, sin, with_ctx=not last, **geo)
    u, w, qe, kdt, qk, eg = _dn_prep(p, ps, dn_conv_w[l], dn_a_log[l], dn_dt_bias[l], **geo)
    dn_f, dn_b = _dn_scan(u, w, qe, kdt, qk, eg, **geo)
    lru = [_lru(p, lru_conv_w[l], lru_conv_b[l].reshape(1, -1),
                _block_diag(lru_w_rgate[l, d]).astype(BF16), lru_b_rgate[l, d].reshape(1, -1),
                _block_diag(lru_w_igate[l, d]).astype(BF16), lru_b_igate[l, d].reshape(1, -1),
                lru_lambda[l, d].reshape(1, -1), reverse=bool(d), **geo) for d in range(2)]
    return att, dn_f, dn_b, lru[0], lru[1]


def kernel(x, c, ctx, c_ctx, w_mod, b_mod, norm_mix_g, norm_ffn_g, w_in, w_out, attn_sink,
           dn_conv_w, dn_a_log, dn_dt_bias, dn_norm_g, lru_conv_w, lru_conv_b,
           lru_w_rgate, lru_b_rgate, lru_w_igate, lru_b_igate, lru_lambda,
           w_router, b_router, w_gu, b_gu, w_down, b_down, final_norm_g):
    batch, seq, d = x.shape
    ctx_len = ctx.shape[1]
    depth = w_mod.shape[0]
    nl = batch * seq
    geo = dict(batch=batch, seq=seq, ctx_len=ctx_len)

    c_all = jnp.zeros((8, d), F32).at[:batch].set(c).at[batch].set(c_ctx)
    mod = _modulation(c_all, w_mod, b_mod).reshape(depth, 8, 6, d)
    cos, sin = _rope_tables(seq)
    xa = jnp.concatenate([x.reshape(nl, d), ctx.reshape(batch * ctx_len, d)], axis=0)

    split = COL_LRU_X
    nsmall = 4 * DN_HEADS
    for l in range(depth):
        last = l == depth - 1
        w_main = jnp.concatenate([w_in[l][:, :split], w_in[l][:, split + nsmall:]], axis=1).astype(BF16)
        w_small = jnp.pad(w_in[l][:, split:split + nsmall], ((0, 0), (0, SMALL_W - nsmall))).astype(BF16)
        p, ps = _in_proj(xa, mod, norm_mix_g[l].reshape(1, d), w_main, w_small, l,
                         batch=batch, seq=seq)
        att, dn_f, dn_b, lru_f, lru_b = _mix_inputs(
            p, ps, l, attn_sink, cos, sin, dn_conv_w, dn_a_log, dn_dt_bias, lru_conv_w, lru_conv_b,
            lru_w_rgate, lru_b_rgate, lru_w_igate, lru_b_igate, lru_lambda, last, geo)
        rows = nl if last else xa.shape[0]
        w_r = jnp.pad(w_router[l], ((0, 0), (0, LANES - N_EXPERTS)))
        b_r = jnp.pad(b_router[l], (0, LANES - N_EXPERTS)).reshape(1, LANES)
        x_mid, h_ffn, logits = _out_proj(att, dn_f, dn_b, p, lru_f, lru_b, xa, w_out[l].astype(BF16),
                                         mod, norm_ffn_g[l].reshape(1, d), dn_norm_g[l].reshape(1, -1),
                                         w_r, b_r, l, batch=batch, seq=seq, rows=rows)
        blk_exp, row_word, n_used, row_gate = _route(logits[:, :N_EXPERTS])
        y4 = _moe(h_ffn, blk_exp, row_word, n_used, row_gate, w_gu[l].astype(BF16), b_gu[l],
                  w_down[l].astype(BF16), b_down[l])
        xa = _combine(x_mid, y4, mod, final_norm_g.reshape(1, d), l, batch=batch, seq=seq,
                      rows=rows, final=last)
    return xa.reshape(batch, seq, d)
```
